```python
import jax, jax.numpy as jnp
from jax import lax
import numpy as np

D_MODEL = 2048
BATCH = 4
SEQ = 2048
DEPTH = 1
DEC_BATCH = 32
DEC_SEQ = 1
PAST_LEN = 8192
PAGE_SIZE = 128

D_ATT = D_MODEL // 2
D_LRU = D_MODEL - D_ATT
HEAD_DIM = 128
N_ATT_HEADS = D_ATT // HEAD_DIM
N_LRU_BLOCKS = 8
LRU_BLOCK = D_LRU // N_LRU_BLOCKS
CONV_W = 4
LRU_C = 8.0
D_FF = 4 * D_MODEL
Q_BLOCK = 128
D_IN_COLS = 3 * D_ATT + N_ATT_HEADS + 2 * D_LRU
EPS = 1e-6
NEG_INF = -1e30

kernel_name = 'hymba_rglru_fox_decode_step'


def rms_norm(x, g):
    xf = x.astype(jnp.float32)
    y = xf * lax.rsqrt(jnp.mean(xf * xf, axis=-1, keepdims=True) + EPS)
    return (y * g.astype(jnp.float32)).astype(x.dtype)


def causal_conv(x, buf, w, b):
    T = x.shape[1]
    xp = jnp.concatenate([buf.astype(x.dtype), x], axis=1)
    out = b
    for j in range(CONV_W):
        out = out + w[j] * xp[:, j:j + T]
    return out.astype(x.dtype), xp[:, T:]


def rg_lru(x, h0, wa, ba, wx, bx, lam):
    B, T, _ = x.shape
    xb = x.reshape(B, T, N_LRU_BLOCKS, LRU_BLOCK)
    r = jax.nn.sigmoid((jnp.einsum('btnc,ncd->btnd', xb, wa).reshape(B, T, D_LRU) + ba).astype(jnp.float32))
    i = jax.nn.sigmoid((jnp.einsum('btnc,ncd->btnd', xb, wx).reshape(B, T, D_LRU) + bx).astype(jnp.float32))
    log_a = -LRU_C * r * jax.nn.softplus(-lam.astype(jnp.float32))
    a = jnp.exp(log_a)
    b_in = jnp.sqrt(-jnp.expm1(2.0 * log_a)) * i * x.astype(jnp.float32)

    def step(h, ab):
        a_t, b_t = ab
        h = a_t * h + b_t
        return h, h

    h_T, hs = lax.scan(step, h0.astype(jnp.float32), (a.swapaxes(0, 1), b_in.swapaxes(0, 1)))
    return hs.swapaxes(0, 1).astype(x.dtype), h_T.astype(x.dtype)


def fox_block(q_blk, q_pos, Fq, k_all, v_all, Fk, k_pos):
    s = jnp.einsum('bqhd,bkhd->bhqk', q_blk.astype(jnp.float32), k_all.astype(jnp.float32)) * (HEAD_DIM ** -0.5)
    s = s + (Fq.transpose(0, 2, 1)[:, :, :, None] - Fk.transpose(0, 2, 1)[:, :, None, :])
    mask = k_pos[None, :] <= q_pos[:, None]
    s = jnp.where(mask[None, None], s, NEG_INF)
    p = jax.nn.softmax(s, axis=-1)
    return jnp.einsum('bhqk,bkhd->bqhd', p, v_all.astype(jnp.float32))


def forgetting_attention(q, k_all, v_all, logf_all, n_past):
    B, Tq = q.shape[0], q.shape[1]
    Tk = k_all.shape[1]
    F = jnp.cumsum(logf_all.astype(jnp.float32), axis=1)
    Fq = F[:, n_past:]
    k_pos = jnp.arange(Tk)
    q_pos = n_past + jnp.arange(Tq)
    qb = Q_BLOCK if Tq % Q_BLOCK == 0 else Tq
    nb = Tq // qb
    qs = q.reshape(B, nb, qb, N_ATT_HEADS, HEAD_DIM).swapaxes(0, 1)
    Fqs = Fq.reshape(B, nb, qb, N_ATT_HEADS).swapaxes(0, 1)
    qps = q_pos.reshape(nb, qb)
    out = lax.map(lambda a: fox_block(a[0], a[2], a[1], k_all, v_all, F, k_pos), (qs, Fqs, qps))
    return out.swapaxes(0, 1).reshape(B, Tq, D_ATT).astype(q.dtype)


def hybrid_layer(x, past_k, past_v, past_logf, conv_buf, h0,
                 g_mix, w_in, b_f, w_conv, b_conv, w_gate_a, b_gate_a, w_gate_x, b_gate_x,
                 lru_lambda, g_att_out, g_lru_out, w_out, g_ffn, w_up, w_down):
    B, T, _ = x.shape
    xn = rms_norm(x, g_mix)
    z = jnp.einsum('btd,dc->btc', xn, w_in)
    cuts = [D_ATT, 2 * D_ATT, 3 * D_ATT, 3 * D_ATT + N_ATT_HEADS, 3 * D_ATT + N_ATT_HEADS + D_LRU]
    q, k, v, f_logit, xl, gl = jnp.split(z, cuts, axis=-1)
    q = q.reshape(B, T, N_ATT_HEADS, HEAD_DIM)
    k = k.reshape(B, T, N_ATT_HEADS, HEAD_DIM)
    v = v.reshape(B, T, N_ATT_HEADS, HEAD_DIM)
    logf = jax.nn.log_sigmoid((f_logit + b_f).astype(jnp.float32)).astype(x.dtype)
    if past_k is None:
        n_past = 0
        k_all, v_all, logf_all = k, v, logf
    else:
        n_past = past_k.shape[1]
        k_all = jnp.concatenate([past_k.astype(k.dtype), k], axis=1)
        v_all = jnp.concatenate([past_v.astype(v.dtype), v], axis=1)
        logf_all = jnp.concatenate([past_logf.astype(logf.dtype), logf], axis=1)
    att = forgetting_attention(q, k_all, v_all, logf_all, n_past)

    xc, new_buf = causal_conv(xl, conv_buf, w_conv, b_conv)
    hs, h_T = rg_lru(xc, h0, w_gate_a, b_gate_a, w_gate_x, b_gate_x, lru_lambda)
    lru = jax.nn.gelu(gl) * hs

    mix = jnp.concatenate([rms_norm(att, g_att_out), rms_norm(lru, g_lru_out)], axis=-1)
    x = x + jnp.einsum('btc,cd->btd', mix, w_out)
    hf = jnp.einsum('btd,df->btf', rms_norm(x, g_ffn), w_up)
    x = x + jnp.einsum('btf,fd->btd', jnp.square(jax.nn.relu(hf)), w_down)
    return x, k, v, logf, new_buf, h_T


def setup_inputs(seed: int = 0) -> dict:
    key = jax.random.key(seed)
    ks = jax.random.split(key, 32)
    n_pages = PAST_LEN // PAGE_SIZE
    n_used = DEC_BATCH * n_pages
    n_pool = (n_used * 5) // 4
    f32 = jnp.float32

    def nrm(k, shape, scale):
        return jax.random.normal(k, shape, f32) * scale

    x_prompt = nrm(ks[0], (BATCH, SEQ, D_MODEL), 1.0)
    x_sample = nrm(ks[1], (DEC_BATCH, DEC_SEQ, D_MODEL), 1.0)
    cache_k = nrm(ks[2], (DEPTH, n_pool, PAGE_SIZE, N_ATT_HEADS, HEAD_DIM), 1.0)
    cache_v = nrm(ks[3], (DEPTH, n_pool, PAGE_SIZE, N_ATT_HEADS, HEAD_DIM), 1.0)
    cache_logf = jax.nn.log_sigmoid(3.0 + nrm(ks[4], (DEPTH, n_pool, PAGE_SIZE, N_ATT_HEADS), 0.5))
    state_conv = nrm(ks[5], (DEPTH, DEC_BATCH, CONV_W - 1, D_LRU), 1.0)
    state_h = nrm(ks[6], (DEPTH, DEC_BATCH, D_LRU), 0.5)
    page_table = jax.random.permutation(ks[7], n_pool)[:n_used].reshape(DEC_BATCH, n_pages).astype(jnp.int32)

    g_mix = 1.0 + nrm(ks[8], (DEPTH, D_MODEL), 0.02)
    w_in = nrm(ks[9], (DEPTH, D_MODEL, D_IN_COLS), D_MODEL ** -0.5)
    b_f = 3.0 + nrm(ks[10], (DEPTH, N_ATT_HEADS), 0.3)
    w_conv = nrm(ks[11], (DEPTH, CONV_W, D_LRU), CONV_W ** -0.5)
    b_conv = nrm(ks[12], (DEPTH, D_LRU), 0.02)
    w_gate_a = nrm(ks[13], (DEPTH, N_LRU_BLOCKS, LRU_BLOCK, LRU_BLOCK), LRU_BLOCK ** -0.5)
    b_gate_a = nrm(ks[14], (DEPTH, D_LRU), 0.02)
    w_gate_x = nrm(ks[15], (DEPTH, N_LRU_BLOCKS, LRU_BLOCK, LRU_BLOCK), LRU_BLOCK ** -0.5)
    b_gate_x = nrm(ks[16], (DEPTH, D_LRU), 0.02)
    a_c = jax.random.uniform(ks[17], (DEPTH, D_LRU), f32, 0.9, 0.999)
    s = a_c ** (1.0 / LRU_C)
    lru_lambda = jnp.log(s) - jnp.log1p(-s)
    g_att_out = 1.0 + nrm(ks[18], (DEPTH, D_ATT), 0.02)
    g_lru_out = 1.0 + nrm(ks[19], (DEPTH, D_LRU), 0.02)
    w_out = nrm(ks[20], (DEPTH, D_ATT + D_LRU, D_MODEL), (D_ATT + D_LRU) ** -0.5)
    g_ffn = 1.0 + nrm(ks[21], (DEPTH, D_MODEL), 0.02)
    w_up = nrm(ks[22], (DEPTH, D_MODEL, D_FF), D_MODEL ** -0.5)
    w_down = nrm(ks[23], (DEPTH, D_FF, D_MODEL), D_FF ** -0.5)
    g_final = 1.0 + nrm(ks[24], (D_MODEL,), 0.02)
    return {'x_prompt': x_prompt, 'x_sample': x_sample, 'cache_k': cache_k, 'cache_v': cache_v,
            'cache_logf': cache_logf, 'state_conv': state_conv, 'state_h': state_h,
            'page_table': page_table, 'g_mix': g_mix, 'w_in': w_in, 'b_f': b_f,
            'w_conv': w_conv, 'b_conv': b_conv, 'w_gate_a': w_gate_a, 'b_gate_a': b_gate_a,
            'w_gate_x': w_gate_x, 'b_gate_x': b_gate_x, 'lru_lambda': lru_lambda,
            'g_att_out': g_att_out, 'g_lru_out': g_lru_out, 'w_out': w_out, 'g_ffn': g_ffn,
            'w_up': w_up, 'w_down': w_down, 'g_final': g_final}


def reference(x_prompt, x_sample, cache_k, cache_v, cache_logf, state_conv, state_h, page_table,
              g_mix, w_in, b_f, w_conv, b_conv, w_gate_a, b_gate_a, w_gate_x, b_gate_x,
              lru_lambda, g_att_out, g_lru_out, w_out, g_ffn, w_up, w_down, g_final):
    n_pages = page_table.shape[1]
    past_len = n_pages * cache_k.shape[2]
    xp, xs = x_prompt, x_sample
    pk, pv, pf, pc, ph = [], [], [], [], []
    sk, sv, sf, sc, sh = [], [], [], [], []
    for l in range(DEPTH):
        w = (g_mix[l], w_in[l], b_f[l], w_conv[l], b_conv[l], w_gate_a[l], b_gate_a[l],
             w_gate_x[l], b_gate_x[l], lru_lambda[l], g_att_out[l], g_lru_out[l], w_out[l],
             g_ffn[l], w_up[l], w_down[l])
        zero_buf = jnp.zeros((xp.shape[0], CONV_W - 1, D_LRU), xp.dtype)
        zero_h = jnp.zeros((xp.shape[0], D_LRU), xp.dtype)
        xp, k1, v1, f1, c1, h1 = hybrid_layer(xp, None, None, None, zero_buf, zero_h, *w)
        dec_b = xs.shape[0]
        past_k = cache_k[l][page_table].reshape(dec_b, past_len, N_ATT_HEADS, HEAD_DIM)
        past_v = cache_v[l][page_table].reshape(dec_b, past_len, N_ATT_HEADS, HEAD_DIM)
        past_f = cache_logf[l][page_table].reshape(dec_b, past_len, N_ATT_HEADS)
        xs, k2, v2, f2, c2, h2 = hybrid_layer(xs, past_k, past_v, past_f, state_conv[l], state_h[l], *w)
        pk.append(k1); pv.append(v1); pf.append(f1); pc.append(c1); ph.append(h1)
        sk.append(k2); sv.append(v2); sf.append(f2); sc.append(c2); sh.append(h2)
    y_prompt = rms_norm(xp, g_final)
    y_sample = rms_norm(xs, g_final)
    return (y_prompt, y_sample,
            jnp.stack(pk), jnp.stack(pv), jnp.stack(pf), jnp.stack(pc), jnp.stack(ph),
            jnp.stack(sk), jnp.stack(sv), jnp.stack(sf), jnp.stack(sc), jnp.stack(sh))
```

```python
import functools

import jax
import jax.numpy as jnp
from jax import lax
from jax.experimental import pallas as pl
from jax.experimental.pallas import tpu as pltpu

D_MODEL = 2048
D_ATT = 1024
D_LRU = 1024
HEAD_DIM = 128
N_HEADS = 8
N_LRU_BLOCKS = 8
LRU_BLOCK = 128
CONV_W = 4
LRU_C = 8.0
D_FF = 4 * D_MODEL
EPS = 1e-6
NEG_INF = -1e30
N_MAIN_COLS = 3 * D_ATT + 2 * D_LRU
F_ROWS = 16
SUBLANES = 8
LANES = 128
MIB = 1024 * 1024

f32 = jnp.float32
bf16 = jnp.bfloat16


def _rms(x, g):
    return x * lax.rsqrt(jnp.mean(x * x, axis=-1, keepdims=True) + EPS) * g


def _softplus(x):
    return jnp.maximum(x, 0.0) + jnp.log1p(jnp.exp(-jnp.abs(x)))


def _gelu_tanh(x):
    c = 0.7978845608028654
    return x * (0.5 * (1.0 + jnp.tanh(c * (x + 0.044715 * (x * x * x)))))


def _split3(x):
    hi = x.astype(bf16)
    r1 = x - hi.astype(f32)
    mid = r1.astype(bf16)
    lo = (r1 - mid.astype(f32)).astype(bf16)
    return hi, mid, lo


def _params(sem, vmem_mib):
    return pltpu.CompilerParams(dimension_semantics=sem, vmem_limit_bytes=vmem_mib * MIB)


def _in_proj_kernel(x_ref, g_ref, w_ref, wft_ref, bf_ref,
                    q_ref, k_ref, v_ref, kb_ref, vb_ref, xl_ref, gl_ref, lft_ref):
    xn = _rms(x_ref[...], g_ref[...]).astype(bf16)

    def cols(c):
        return jnp.dot(xn, w_ref[:, c * D_ATT:(c + 1) * D_ATT], preferred_element_type=f32)

    q_ref[...] = (cols(0) * (HEAD_DIM ** -0.5)).astype(q_ref.dtype)
    k = cols(1)
    k_ref[...] = k
    kb_ref[...] = k.astype(bf16)
    v = cols(2)
    v_ref[...] = v
    vb_ref[...] = v.astype(bf16)
    xl_ref[...] = cols(3)
    gl_ref[...] = cols(4)
    t = lax.dot_general(wft_ref[...], xn, (((1,), (1,)), ((), ())),
                        preferred_element_type=f32) + bf_ref[...]
    lft_ref[...] = jnp.minimum(t, 0.0) - jnp.log1p(jnp.exp(-jnp.abs(t)))


def _in_proj(x, g_mix, w_main, wft, b_f, *, tm, q_dtype):
    m = x.shape[0]
    bfb = jnp.broadcast_to(jnp.pad(b_f, (0, F_ROWS - N_HEADS))[:, None], (F_ROWS, tm)).astype(f32)
    row = lambda i: (i, 0)
    const = lambda i: (0, 0)
    wide = pl.BlockSpec((tm, D_ATT), row)
    out_shape = (
        jax.ShapeDtypeStruct((m, D_ATT), q_dtype),
        jax.ShapeDtypeStruct((m, D_ATT), f32), jax.ShapeDtypeStruct((m, D_ATT), f32),
        jax.ShapeDtypeStruct((m, D_ATT), bf16), jax.ShapeDtypeStruct((m, D_ATT), bf16),
        jax.ShapeDtypeStruct((m, D_LRU), f32), jax.ShapeDtypeStruct((m, D_LRU), f32),
        jax.ShapeDtypeStruct((F_ROWS, m), f32),
    )
    return pl.pallas_call(
        _in_proj_kernel,
        grid=(m // tm,),
        in_specs=[
            pl.BlockSpec((tm, D_MODEL), row),
            pl.BlockSpec((1, D_MODEL), const),
            pl.BlockSpec((D_MODEL, N_MAIN_COLS), const, pipeline_mode=pl.Buffered(1)),
            pl.BlockSpec((F_ROWS, D_MODEL), const),
            pl.BlockSpec((F_ROWS, tm), const),
        ],
        out_specs=(wide,) * 7 + (pl.BlockSpec((F_ROWS, tm), lambda i: (0, i)),),
        out_shape=out_shape,
        compiler_params=_params(("parallel",), 52),
        name="in_proj",
    )(x, g_mix.reshape(1, D_MODEL), w_main, wft, bfb)


def _lru_gates(xc, wa_ref, wx_ref, ba, bx, lam):
    xcb = xc.astype(bf16)
    r_parts, i_parts = [], []
    for n in range(N_LRU_BLOCKS):
        blk = xcb[:, n * LRU_BLOCK:(n + 1) * LRU_BLOCK]
        r_parts.append(jnp.dot(blk, wa_ref[n], preferred_element_type=f32))
        i_parts.append(jnp.dot(blk, wx_ref[n], preferred_element_type=f32))
    r = jax.nn.sigmoid(jnp.concatenate(r_parts, axis=1) + ba)
    i = jax.nn.sigmoid(jnp.concatenate(i_parts, axis=1) + bx)
    log_a = (-LRU_C * r) * _softplus(-lam)
    a = jnp.exp(log_a)
    th = jnp.tanh(log_a)
    b = jnp.sqrt((-2.0 * th) / (1.0 - th)) * i * xc
    return a, b


def _lru_kernel(xl_ref, gl_ref, wc_ref, bc_ref, wa_ref, wx_ref, ba_ref, bx_ref, lam_ref, g_ref,
                out_ref, ht_ref, ext_ref, h_ref, hs_ref):
    c = pl.program_id(1)
    tb = xl_ref.shape[0]

    @pl.when(c == 0)
    def _():
        ext_ref[0:SUBLANES, :] = jnp.zeros((SUBLANES, D_LRU), f32)
        h_ref[...] = jnp.zeros((SUBLANES, D_LRU), f32)

    @pl.when(c > 0)
    def _():
        ext_ref[0:SUBLANES, :] = ext_ref[tb:tb + SUBLANES, :]

    x = xl_ref[...]
    ext_ref[SUBLANES:tb + SUBLANES, :] = x
    wc = wc_ref[...]
    xc = bc_ref[...] + wc[0:1] * ext_ref[5:5 + tb, :]
    xc = xc + wc[1:2] * ext_ref[6:6 + tb, :]
    xc = xc + wc[2:3] * ext_ref[7:7 + tb, :]
    xc = xc + wc[3:4] * x

    a, b = _lru_gates(xc, wa_ref, wx_ref, ba_ref[...], bx_ref[...], lam_ref[...])

    row = lax.broadcasted_iota(jnp.int32, (tb, D_LRU), 0) & (SUBLANES - 1)
    for s in (1, 2, 4):
        a_sh = pltpu.roll(a, s, 0)
        b_sh = pltpu.roll(b, s, 0)
        take = row >= s
        b = jnp.where(take, a * b_sh + b, b)
        a = jnp.where(take, a * a_sh, a)

    carry = h_ref[...]
    for g in range(tb // SUBLANES):
        sl = slice(g * SUBLANES, (g + 1) * SUBLANES)
        hg = a[sl] * carry + b[sl]
        hs_ref[sl, :] = hg
        carry = jnp.broadcast_to(hg[SUBLANES - 1:SUBLANES, :], (SUBLANES, D_LRU))
    h_ref[...] = carry

    lru = _gelu_tanh(gl_ref[...]) * hs_ref[...]
    out_ref[...] = _rms(lru, g_ref[...]).astype(out_ref.dtype)

    @pl.when(c == pl.num_programs(1) - 1)
    def _():
        ht_ref[0] = carry[0:1]


def _lru(xl, gl, w_conv, b_conv, wa, wx, ba, bx, lam, g_lru, *, n_seq, seq, tb):
    nc = seq // tb
    row = lambda b, c: (b * nc + c, 0)
    const2 = lambda b, c: (0, 0)
    const3 = lambda b, c: (0, 0, 0)
    vec = pl.BlockSpec((1, D_LRU), const2)
    return pl.pallas_call(
        _lru_kernel,
        grid=(n_seq, nc),
        in_specs=[
            pl.BlockSpec((tb, D_LRU), row), pl.BlockSpec((tb, D_LRU), row),
            pl.BlockSpec((CONV_W, D_LRU), const2), vec,
            pl.BlockSpec((N_LRU_BLOCKS, LRU_BLOCK, LRU_BLOCK), const3),
            pl.BlockSpec((N_LRU_BLOCKS, LRU_BLOCK, LRU_BLOCK), const3),
            vec, vec, vec, vec,
        ],
        out_specs=(pl.BlockSpec((tb, D_LRU), row),
                   pl.BlockSpec((1, 1, D_LRU), lambda b, c: (b, 0, 0))),
        out_shape=(jax.ShapeDtypeStruct((n_seq * seq, D_LRU), bf16),
                   jax.ShapeDtypeStruct((n_seq, 1, D_LRU), f32)),
        scratch_shapes=[pltpu.VMEM((tb + SUBLANES, D_LRU), f32),
                        pltpu.VMEM((SUBLANES, D_LRU), f32),
                        pltpu.VMEM((tb, D_LRU), f32)],
        compiler_params=_params(("parallel", "arbitrary"), 40),
        name="lru",
    )(xl, gl, w_conv, b_conv.reshape(1, D_LRU), wa, wx, ba.reshape(1, D_LRU),
      bx.reshape(1, D_LRU), lam.reshape(1, D_LRU), g_lru.reshape(1, D_LRU))


def _lru_step_kernel(xl_ref, gl_ref, s0_ref, s1_ref, s2_ref, h0_ref, wc_ref, bc_ref, wa_ref, wx_ref,
                     ba_ref, bx_ref, lam_ref, g_ref, out_ref, h_ref):
    wc = wc_ref[...]
    xc = bc_ref[...] + wc[0:1] * s0_ref[...]
    xc = xc + wc[1:2] * s1_ref[...]
    xc = xc + wc[2:3] * s2_ref[...]
    xc = xc + wc[3:4] * xl_ref[...]
    a, b = _lru_gates(xc, wa_ref, wx_ref, ba_ref[...], bx_ref[...], lam_ref[...])
    h = a * h0_ref[...] + b
    h_ref[...] = h
    out_ref[...] = _rms(_gelu_tanh(gl_ref[...]) * h, g_ref[...]).astype(out_ref.dtype)


def _lru_step(xl, gl, s0, s1, s2, h0, w_conv, b_conv, wa, wx, ba, bx, lam, g_lru):
    n = xl.shape[0]
    return pl.pallas_call(
        _lru_step_kernel,
        out_shape=(jax.ShapeDtypeStruct((n, D_LRU), bf16), jax.ShapeDtypeStruct((n, D_LRU), f32)),
        name="lru_step",
    )(xl, gl, s0, s1, s2, h0, w_conv, b_conv.reshape(1, D_LRU), wa, wx, ba.reshape(1, D_LRU),
      bx.reshape(1, D_LRU), lam.reshape(1, D_LRU), g_lru.reshape(1, D_LRU))


def _attn_kernel(q_ref, k_ref, v_ref, lft_ref, g_ref, o_ref, nf_ref, att_ref, *, blk):
    qi = pl.program_id(1)
    n_blk = nf_ref.shape[0]

    @pl.when(qi == 0)
    def _():
        r = lax.broadcasted_iota(jnp.int32, (blk, 2 * blk), 0)
        c = lax.broadcasted_iota(jnp.int32, (blk, 2 * blk), 1)
        tri_ones = jnp.where((r <= c) | (c >= blk), 1.0, 0.0).astype(bf16)
        carry = jnp.zeros((F_ROWS, blk), f32)
        for j in range(n_blk):
            hi, mid, lo = _split3(lft_ref[:, j * blk:(j + 1) * blk])
            res = (jnp.dot(hi, tri_ones, preferred_element_type=f32)
                   + jnp.dot(mid, tri_ones, preferred_element_type=f32)
                   + jnp.dot(lo, tri_ones, preferred_element_type=f32))
            cum = carry + res[:, :blk]
            nf_ref[j] = -cum[0:N_HEADS]
            carry = carry + res[:, blk:]

    rows = lax.broadcasted_iota(jnp.int32, (blk, blk), 0)
    colsi = lax.broadcasted_iota(jnp.int32, (blk, blk), 1)
    causal = colsi <= rows

    for h in range(N_HEADS):
        hs = slice(h * HEAD_DIM, (h + 1) * HEAD_DIM)
        q = q_ref[:, hs]

        def scores(j):
            off = pl.multiple_of(j * blk, blk)
            kj = k_ref[pl.ds(off, blk), hs]
            s = lax.dot_general(q, kj, (((1,), (1,)), ((), ())), preferred_element_type=f32)
            return s + nf_ref[j][h:h + 1, :], off

        def update(carry, s, off):
            m, l, acc = carry
            m_new = jnp.maximum(m, jnp.max(s, axis=-1, keepdims=True))
            alpha = jnp.exp(m - m_new)
            p = jnp.exp(s - m_new)
            l = alpha * l + jnp.sum(p, axis=-1, keepdims=True)
            acc = alpha * acc + jnp.dot(p.astype(bf16), v_ref[pl.ds(off, blk), hs],
                                        preferred_element_type=f32)
            return m_new, l, acc

        def body(j, carry):
            s, off = scores(j)
            return update(carry, s, off)

        init = (jnp.full((blk, 1), NEG_INF, f32), jnp.zeros((blk, 1), f32),
                jnp.zeros((blk, HEAD_DIM), f32))
        carry = lax.fori_loop(0, qi, body, init)
        s, off = scores(qi)
        m, l, acc = update(carry, jnp.where(causal, s, NEG_INF), off)
        att_ref[:, hs] = acc / l

    o_ref[...] = _rms(att_ref[...], g_ref[...]).astype(o_ref.dtype)


def _attn(q, kb, vb, lft, g_att, *, n_seq, seq, blk):
    nq = seq // blk
    return pl.pallas_call(
        functools.partial(_attn_kernel, blk=blk),
        grid=(n_seq, nq),
        in_specs=[
            pl.BlockSpec((blk, D_ATT), lambda b, i: (b * nq + i, 0)),
            pl.BlockSpec((seq, D_ATT), lambda b, i: (b, 0)),
            pl.BlockSpec((seq, D_ATT), lambda b, i: (b, 0)),
            pl.BlockSpec((F_ROWS, seq), lambda b, i: (0, b)),
            pl.BlockSpec((1, D_ATT), lambda b, i: (0, 0)),
        ],
        out_specs=pl.BlockSpec((blk, D_ATT), lambda b, i: (b * nq + i, 0)),
        out_shape=jax.ShapeDtypeStruct((n_seq * seq, D_ATT), bf16),
        scratch_shapes=[pltpu.VMEM((nq, N_HEADS, blk), f32), pltpu.VMEM((blk, D_ATT), f32)],
        compiler_params=_params(("parallel", "arbitrary"), 40),
        name="attn",
    )(q, kb, vb, lft, g_att.reshape(1, D_ATT))


DECODE_CHUNK = 32


def _decode_kernel(pt_ref, q_ref, kn_ref, vn_ref, lfn_ref, k_ref, v_ref, lf_ref, g_ref,
                   o_ref, m_ref, l_ref, acc_ref, car_ref):
    del pt_ref
    p = pl.program_id(1)
    page = k_ref.shape[1]

    @pl.when(p == 0)
    def _():
        m_ref[...] = jnp.full(m_ref.shape, NEG_INF, f32)
        l_ref[...] = jnp.zeros(l_ref.shape, f32)
        acc_ref[...] = jnp.zeros(acc_ref.shape, f32)
        car_ref[...] = jnp.zeros(car_ref.shape, f32)

    lf = lf_ref[0]
    r = lax.broadcasted_iota(jnp.int32, (page, 2 * page), 0)
    c = lax.broadcasted_iota(jnp.int32, (page, 2 * page), 1)
    tri_ones = jnp.where((r <= c) | (c >= page), 1.0, 0.0).astype(bf16)
    hi, mid, lo = _split3(jnp.concatenate([lf, jnp.zeros_like(lf)], axis=0))
    res = (jnp.dot(hi, tri_ones, preferred_element_type=f32)
           + jnp.dot(mid, tri_ones, preferred_element_type=f32)
           + jnp.dot(lo, tri_ones, preferred_element_type=f32))
    cum = res[0:N_HEADS, :page]
    tot = res[0:N_HEADS, page:]
    lane = lax.broadcasted_iota(jnp.int32, (N_HEADS, LANES), 1)
    q = q_ref[0]
    car = car_ref[...]

    m_run = m_ref[...]
    l_run = l_ref[...]
    a_run = acc_ref[...]
    for c0 in range(0, page, DECODE_CHUNK):
        pos = range(c0, c0 + DECODE_CHUNK)
        sc = [jnp.sum(k_ref[0, s] * q - jnp.where(lane == s, cum, 0.0), axis=-1, keepdims=True)
              for s in pos]
        m_new = jnp.maximum(m_run, functools.reduce(jnp.maximum, sc) - car)
        alpha = jnp.exp(m_run - m_new)
        shift = m_new + car
        pr = [jnp.exp(x - shift) for x in sc]
        pv = [pr[i] * v_ref[0, s] for i, s in enumerate(pos)]
        while len(pr) > 1:
            pr = [a + b for a, b in zip(pr[0::2], pr[1::2])]
            pv = [a + b for a, b in zip(pv[0::2], pv[1::2])]
        l_run = l_run * alpha + pr[0]
        a_run = a_run * alpha + pv[0]
        m_run = m_new
    m_ref[...] = m_run
    l_ref[...] = jnp.broadcast_to(l_run, l_ref.shape)
    acc_ref[...] = a_run
    car_ref[...] = car + tot

    @pl.when(p == pl.num_programs(1) - 1)
    def _():
        f_new = car + tot + lfn_ref[0]
        s_new = jnp.sum(kn_ref[0] * q, axis=-1, keepdims=True) - f_new
        m_all = jnp.maximum(m_run, s_new)
        w = jnp.exp(m_run - m_all)
        w_new = jnp.exp(s_new - m_all)
        o = (a_run * w + w_new * vn_ref[0]) / (l_run * w + w_new)
        ms = jnp.sum(jnp.sum(o * o, axis=-1, keepdims=True), axis=0, keepdims=True) / D_ATT
        o_ref[0] = o * lax.rsqrt(ms + EPS) * g_ref[...]


def _decode(page_table, q, k_new, v_new, lf_new, cache_k, cache_v, cache_lf_t, g_att):
    n_seq, n_pages = page_table.shape
    page = cache_k.shape[1]
    pt = page_table.reshape(-1)
    tile = (1, N_HEADS, HEAD_DIM)
    tok = pl.BlockSpec(tile, lambda b, p, pt: (b, 0, 0))
    state = pltpu.VMEM((N_HEADS, LANES), f32)
    grid_spec = pltpu.PrefetchScalarGridSpec(
        num_scalar_prefetch=1,
        grid=(n_seq, n_pages),
        in_specs=[tok, tok, tok, tok,
                  pl.BlockSpec((1, page) + tile[1:], lambda b, p, pt: (pt[b * n_pages + p], 0, 0, 0)),
                  pl.BlockSpec((1, page) + tile[1:], lambda b, p, pt: (pt[b * n_pages + p], 0, 0, 0)),
                  pl.BlockSpec((1, N_HEADS, page), lambda b, p, pt: (pt[b * n_pages + p], 0, 0)),
                  pl.BlockSpec((N_HEADS, HEAD_DIM), lambda b, p, pt: (0, 0))],
        out_specs=tok,
        scratch_shapes=[state, state, state, state],
    )
    return pl.pallas_call(
        _decode_kernel,
        grid_spec=grid_spec,
        out_shape=jax.ShapeDtypeStruct((n_seq,) + tile[1:], f32),
        compiler_params=_params(("parallel", "arbitrary"), 32),
        name="decode",
    )(pt, q, k_new, v_new, lf_new, cache_k, cache_v, cache_lf_t, g_att.reshape(N_HEADS, HEAD_DIM))


def _out_proj_kernel(att_ref, lru_ref, x_ref, w_ref, g_ref, x1_ref, xn_ref):
    mix = jnp.concatenate([att_ref[...].astype(bf16), lru_ref[...].astype(bf16)], axis=1)
    x1 = x_ref[...] + jnp.dot(mix, w_ref[...], preferred_element_type=f32)
    x1_ref[...] = x1
    xn_ref[...] = _rms(x1, g_ref[...]).astype(bf16)


def _out_proj(att_n, lru_n, x, w_out, g_ffn, *, tm):
    m = x.shape[0]
    row = lambda i: (i, 0)
    const = lambda i: (0, 0)
    return pl.pallas_call(
        _out_proj_kernel,
        grid=(m // tm,),
        in_specs=[pl.BlockSpec((tm, D_ATT), row), pl.BlockSpec((tm, D_LRU), row),
                  pl.BlockSpec((tm, D_MODEL), row),
                  pl.BlockSpec((D_MODEL, D_MODEL), const, pipeline_mode=pl.Buffered(1)),
                  pl.BlockSpec((1, D_MODEL), const)],
        out_specs=(pl.BlockSpec((tm, D_MODEL), row), pl.BlockSpec((tm, D_MODEL), row)),
        out_shape=(jax.ShapeDtypeStruct((m, D_MODEL), f32), jax.ShapeDtypeStruct((m, D_MODEL), bf16)),
        compiler_params=_params(("parallel",), 48),
        name="out_proj",
    )(att_n, lru_n, x, w_out, g_ffn.reshape(1, D_MODEL))


def _ffn_kernel(xn_ref, x1_ref, wu_ref, wd_ref, g_ref, o_ref, acc_ref):
    f = pl.program_id(1)

    @pl.when(f == 0)
    def _():
        acc_ref[...] = jnp.zeros(acc_ref.shape, f32)

    hf = jnp.dot(xn_ref[...], wu_ref[...], preferred_element_type=f32)
    act = jnp.square(jnp.maximum(hf, 0.0)).astype(bf16)
    acc_ref[...] += jnp.dot(act, wd_ref[...], preferred_element_type=f32)

    @pl.when(f == pl.num_programs(1) - 1)
    def _():
        o_ref[...] = _rms(x1_ref[...] + acc_ref[...], g_ref[...])


def _ffn(xn, x1, w_up, w_down, g_final, *, tm, tf):
    m = x1.shape[0]
    row = lambda i, f: (i, 0)
    return pl.pallas_call(
        _ffn_kernel,
        grid=(m // tm, D_FF // tf),
        in_specs=[pl.BlockSpec((tm, D_MODEL), row, pipeline_mode=pl.Buffered(1)),
                  pl.BlockSpec((tm, D_MODEL), row, pipeline_mode=pl.Buffered(1)),
                  pl.BlockSpec((D_MODEL, tf), lambda i, f: (0, f)),
                  pl.BlockSpec((tf, D_MODEL), lambda i, f: (f, 0)),
                  pl.BlockSpec((1, D_MODEL), lambda i, f: (0, 0))],
        out_specs=pl.BlockSpec((tm, D_MODEL), row),
        out_shape=jax.ShapeDtypeStruct((m, D_MODEL), f32),
        scratch_shapes=[pltpu.VMEM((tm, D_MODEL), f32)],
        compiler_params=_params(("parallel", "arbitrary"), 56),
        name="ffn",
    )(xn, x1, w_up, w_down, g_final.reshape(1, D_MODEL))


def kernel(x_prompt, x_sample, cache_k, cache_v, cache_logf, state_conv, state_h, page_table,
           g_mix, w_in, b_f, w_conv, b_conv, w_gate_a, b_gate_a, w_gate_x, b_gate_x,
           lru_lambda, g_att_out, g_lru_out, w_out, g_ffn, w_up, w_down, g_final):
    depth = w_in.shape[0]
    n_seq, seq, _ = x_prompt.shape
    n_dec = x_sample.shape[0]
    assert depth == 1 and x_sample.shape[1] == 1

    l = 0
    c_f = 3 * D_ATT
    w_main = jnp.concatenate([w_in[l][:, :c_f], w_in[l][:, c_f + N_HEADS:]], axis=1).astype(bf16)
    wft = jnp.pad(w_in[l][:, c_f:c_f + N_HEADS].T, ((0, F_ROWS - N_HEADS), (0, 0))).astype(bf16)
    wa = w_gate_a[l].astype(bf16)
    wx = w_gate_x[l].astype(bf16)
    wo = w_out[l].astype(bf16)
    wu = w_up[l].astype(bf16)
    wd = w_down[l].astype(bf16)
    lru_w = (w_conv[l], b_conv[l], wa, wx, b_gate_a[l], b_gate_x[l], lru_lambda[l], g_lru_out[l])

    xp = x_prompt.reshape(n_seq * seq, D_MODEL)
    q, k, v, kb, vb, xl, gl, lft = _in_proj(xp, g_mix[l], w_main, wft, b_f[l], tm=256, q_dtype=bf16)
    lru_n, h_t = _lru(xl, gl, *lru_w, n_seq=n_seq, seq=seq, tb=256)
    att_n = _attn(q, kb, vb, lft, g_att_out[l], n_seq=n_seq, seq=seq, blk=256)
    x1, xn = _out_proj(att_n, lru_n, xp, wo, g_ffn[l], tm=512)
    y_prompt = _ffn(xn, x1, wu, wd, g_final, tm=1024, tf=512).reshape(n_seq, seq, D_MODEL)

    new_k_p = k.reshape(1, n_seq, seq, N_HEADS, HEAD_DIM)
    new_v_p = v.reshape(1, n_seq, seq, N_HEADS, HEAD_DIM)
    new_f_p = lft[:N_HEADS].T.reshape(1, n_seq, seq, N_HEADS)
    new_c_p = xl.reshape(n_seq, seq, D_LRU)[:, seq - (CONV_W - 1):, :][None]
    new_h_p = h_t.reshape(1, n_seq, D_LRU)

    xs = x_sample.reshape(n_dec, D_MODEL)
    qs, ks, vs, _, _, xls, gls, lfts = _in_proj(xs, g_mix[l], w_main, wft, b_f[l], tm=n_dec, q_dtype=f32)
    lf_s = lfts[:N_HEADS].T
    heads = lambda t: t.reshape(n_dec, N_HEADS, HEAD_DIM)
    lf_wide = jnp.broadcast_to(lf_s[:, :, None], (n_dec, N_HEADS, HEAD_DIM))
    att_s = _decode(page_table, heads(qs), heads(ks), heads(vs), lf_wide, cache_k[l], cache_v[l],
                    jnp.swapaxes(cache_logf[l], 1, 2), g_att_out[l]).reshape(n_dec, D_ATT)
    sc = state_conv[l]
    lru_s, h_s = _lru_step(xls, gls, sc[:, 0], sc[:, 1], sc[:, 2], state_h[l], *lru_w)
    x1s, xns = _out_proj(att_s, lru_s, xs, wo, g_ffn[l], tm=n_dec)
    y_sample = _ffn(xns, x1s, wu, wd, g_final, tm=n_dec, tf=512).reshape(n_dec, 1, D_MODEL)

    new_k_s = ks.reshape(1, n_dec, 1, N_HEADS, HEAD_DIM)
    new_v_s = vs.reshape(1, n_dec, 1, N_HEADS, HEAD_DIM)
    new_f_s = lf_s.reshape(1, n_dec, 1, N_HEADS)
    new_c_s = jnp.stack([sc[:, 1], sc[:, 2], xls], axis=1)[None]
    new_h_s = h_s[None]

    return (y_prompt, y_sample, new_k_p, new_v_p, new_f_p, new_c_p, new_h_p,
            new_k_s, new_v_s, new_f_s, new_c_s, new_h_s)
```

```python
import functools

import jax
import jax.numpy as jnp
from jax import lax
from jax.experimental import pallas as pl
from jax.experimental.pallas import tpu as pltpu

D_MODEL = 2048
D_ATT = 1024
D_LRU = 1024
HEAD_DIM = 128
N_HEADS = 8
N_LRU_BLOCKS = 8
LRU_BLOCK = 128
CONV_W = 4
LRU_C = 8.0
D_FF = 4 * D_MODEL
EPS = 1e-6
NEG_INF = -1e30
F_ROWS = 16
SUBLANES = 8
LANES = 128
MIB = 1024 * 1024

f32 = jnp.float32
bf16 = jnp.bfloat16


def _rms(x, g):
    return x * lax.rsqrt(jnp.mean(x * x, axis=-1, keepdims=True) + EPS) * g


def _softplus(x):
    return jnp.maximum(x, 0.0) + jnp.log1p(jnp.exp(-jnp.abs(x)))


def _gelu_tanh(x):
    c = 0.7978845608028654
    return x * (0.5 * (1.0 + jnp.tanh(c * (x + 0.044715 * (x * x * x)))))


def _split3(x):
    hi = x.astype(bf16)
    r1 = x - hi.astype(f32)
    mid = r1.astype(bf16)
    lo = (r1 - mid.astype(f32)).astype(bf16)
    return hi, mid, lo


def _params(sem, vmem_mib):
    return pltpu.CompilerParams(dimension_semantics=sem, vmem_limit_bytes=vmem_mib * MIB)


def _in_proj_kernel(x_ref, g_ref, wqkv_ref, wlru_ref, wft_ref, bf_ref,
                    q_ref, k_ref, v_ref, kb_ref, vb_ref, xl_ref, gl_ref, lft_ref):
    xn = _rms(x_ref[...], g_ref[...]).astype(bf16)

    def cols(w_ref, c):
        return jnp.dot(xn, w_ref[:, c * D_ATT:(c + 1) * D_ATT], preferred_element_type=f32)

    q_ref[...] = (cols(wqkv_ref, 0) * (HEAD_DIM ** -0.5)).astype(q_ref.dtype)
    k = cols(wqkv_ref, 1)
    k_ref[...] = k
    kb_ref[...] = k.astype(bf16)
    v = cols(wqkv_ref, 2)
    v_ref[...] = v
    vb_ref[...] = v.astype(bf16)
    xl_ref[...] = cols(wlru_ref, 0)
    gl_ref[...] = cols(wlru_ref, 1)
    t = lax.dot_general(wft_ref[...], xn, (((1,), (1,)), ((), ())),
                        preferred_element_type=f32) + bf_ref[...]
    lft_ref[...] = jnp.minimum(t, 0.0) - jnp.log1p(jnp.exp(-jnp.abs(t)))


def _in_proj(x, g_mix, w_qkv, w_lru, wft, b_f, *, tm, q_dtype):
    m = x.shape[0]
    bfb = jnp.broadcast_to(jnp.pad(b_f, (0, F_ROWS - N_HEADS))[:, None], (F_ROWS, tm)).astype(f32)
    row = lambda i: (i, 0)
    const = lambda i: (0, 0)
    wide = pl.BlockSpec((tm, D_ATT), row)
    out_shape = (
        jax.ShapeDtypeStruct((m, D_ATT), q_dtype),
        jax.ShapeDtypeStruct((m, D_ATT), f32), jax.ShapeDtypeStruct((m, D_ATT), f32),
        jax.ShapeDtypeStruct((m, D_ATT), bf16), jax.ShapeDtypeStruct((m, D_ATT), bf16),
        jax.ShapeDtypeStruct((m, D_LRU), f32), jax.ShapeDtypeStruct((m, D_LRU), f32),
        jax.ShapeDtypeStruct((F_ROWS, m), f32),
    )
    return pl.pallas_call(
        _in_proj_kernel,
        grid=(m // tm,),
        in_specs=[
            pl.BlockSpec((tm, D_MODEL), row),
            pl.BlockSpec((1, D_MODEL), const),
            pl.BlockSpec((D_MODEL, 3 * D_ATT), const, pipeline_mode=pl.Buffered(1)),
            pl.BlockSpec((D_MODEL, 2 * D_LRU), const, pipeline_mode=pl.Buffered(1)),
            pl.BlockSpec((F_ROWS, D_MODEL), const),
            pl.BlockSpec((F_ROWS, tm), const),
        ],
        out_specs=(wide,) * 7 + (pl.BlockSpec((F_ROWS, tm), lambda i: (0, i)),),
        out_shape=out_shape,
        compiler_params=_params(("parallel",), 52),
        name="in_proj",
    )(x, g_mix.reshape(1, D_MODEL), w_qkv, w_lru, wft, bfb)


def _lru_gates(xc, wa_ref, wx_ref, ba, bx, lam):
    xcb = xc.astype(bf16)
    r_parts, i_parts = [], []
    for n in range(N_LRU_BLOCKS):
        blk = xcb[:, n * LRU_BLOCK:(n + 1) * LRU_BLOCK]
        r_parts.append(jnp.dot(blk, wa_ref[n], preferred_element_type=f32))
        i_parts.append(jnp.dot(blk, wx_ref[n], preferred_element_type=f32))
    r = jax.nn.sigmoid(jnp.concatenate(r_parts, axis=1) + ba)
    i = jax.nn.sigmoid(jnp.concatenate(i_parts, axis=1) + bx)
    log_a = (-LRU_C * r) * _softplus(-lam)
    a = jnp.exp(log_a)
    th = jnp.tanh(log_a)
    b = jnp.sqrt((-2.0 * th) / (1.0 - th)) * i * xc
    return a, b


def _lru_kernel(xl_ref, gl_ref, wc_ref, bc_ref, wa_ref, wx_ref, ba_ref, bx_ref, lam_ref, g_ref,
                out_ref, ht_ref, ext_ref, h_ref, hs_ref):
    c = pl.program_id(1)
    tb = xl_ref.shape[0]

    @pl.when(c == 0)
    def _():
        ext_ref[0:SUBLANES, :] = jnp.zeros((SUBLANES, D_LRU), f32)
        h_ref[...] = jnp.zeros((SUBLANES, D_LRU), f32)

    @pl.when(c > 0)
    def _():
        ext_ref[0:SUBLANES, :] = ext_ref[tb:tb + SUBLANES, :]

    x = xl_ref[...]
    ext_ref[SUBLANES:tb + SUBLANES, :] = x
    wc = wc_ref[...]
    xc = bc_ref[...] + wc[0:1] * ext_ref[5:5 + tb, :]
    xc = xc + wc[1:2] * ext_ref[6:6 + tb, :]
    xc = xc + wc[2:3] * ext_ref[7:7 + tb, :]
    xc = xc + wc[3:4] * x

    a, b = _lru_gates(xc, wa_ref, wx_ref, ba_ref[...], bx_ref[...], lam_ref[...])

    row = lax.broadcasted_iota(jnp.int32, (tb, D_LRU), 0) & (SUBLANES - 1)
    for s in (1, 2, 4):
        a_sh = pltpu.roll(a, s, 0)
        b_sh = pltpu.roll(b, s, 0)
        take = row >= s
        b = jnp.where(take, a * b_sh + b, b)
        a = jnp.where(take, a * a_sh, a)

    carry = h_ref[...]
    for g in range(tb // SUBLANES):
        sl = slice(g * SUBLANES, (g + 1) * SUBLANES)
        hg = a[sl] * carry + b[sl]
        hs_ref[sl, :] = hg
        carry = jnp.broadcast_to(hg[SUBLANES - 1:SUBLANES, :], (SUBLANES, D_LRU))
    h_ref[...] = carry

    lru = _gelu_tanh(gl_ref[...]) * hs_ref[...]
    out_ref[...] = _rms(lru, g_ref[...]).astype(out_ref.dtype)

    @pl.when(c == pl.num_programs(1) - 1)
    def _():
        ht_ref[0] = carry[0:1]


def _lru(xl, gl, w_conv, b_conv, wa, wx, ba, bx, lam, g_lru, *, n_seq, seq, tb):
    nc = seq // tb
    row = lambda b, c: (b * nc + c, 0)
    const2 = lambda b, c: (0, 0)
    const3 = lambda b, c: (0, 0, 0)
    vec = pl.BlockSpec((1, D_LRU), const2)
    return pl.pallas_call(
        _lru_kernel,
        grid=(n_seq, nc),
        in_specs=[
            pl.BlockSpec((tb, D_LRU), row), pl.BlockSpec((tb, D_LRU), row),
            pl.BlockSpec((CONV_W, D_LRU), const2), vec,
            pl.BlockSpec((N_LRU_BLOCKS, LRU_BLOCK, LRU_BLOCK), const3),
            pl.BlockSpec((N_LRU_BLOCKS, LRU_BLOCK, LRU_BLOCK), const3),
            vec, vec, vec, vec,
        ],
        out_specs=(pl.BlockSpec((tb, D_LRU), row),
                   pl.BlockSpec((1, 1, D_LRU), lambda b, c: (b, 0, 0))),
        out_shape=(jax.ShapeDtypeStruct((n_seq * seq, D_LRU), bf16),
                   jax.ShapeDtypeStruct((n_seq, 1, D_LRU), f32)),
        scratch_shapes=[pltpu.VMEM((tb + SUBLANES, D_LRU), f32),
                        pltpu.VMEM((SUBLANES, D_LRU), f32),
                        pltpu.VMEM((tb, D_LRU), f32)],
        compiler_params=_params(("parallel", "arbitrary"), 40),
        name="lru",
    )(xl, gl, w_conv, b_conv.reshape(1, D_LRU), wa, wx, ba.reshape(1, D_LRU),
      bx.reshape(1, D_LRU), lam.reshape(1, D_LRU), g_lru.reshape(1, D_LRU))


def _lru_step_kernel(xl_ref, gl_ref, s0_ref, s1_ref, s2_ref, h0_ref, wc_ref, bc_ref, wa_ref, wx_ref,
                     ba_ref, bx_ref, lam_ref, g_ref, out_ref, h_ref):
    wc = wc_ref[...]
    xc = bc_ref[...] + wc[0:1] * s0_ref[...]
    xc = xc + wc[1:2] * s1_ref[...]
    xc = xc + wc[2:3] * s2_ref[...]
    xc = xc + wc[3:4] * xl_ref[...]
    a, b = _lru_gates(xc, wa_ref, wx_ref, ba_ref[...], bx_ref[...], lam_ref[...])
    h = a * h0_ref[...] + b
    h_ref[...] = h
    out_ref[...] = _rms(_gelu_tanh(gl_ref[...]) * h, g_ref[...]).astype(out_ref.dtype)


def _lru_step(xl, gl, s0, s1, s2, h0, w_conv, b_conv, wa, wx, ba, bx, lam, g_lru):
    n = xl.shape[0]
    return pl.pallas_call(
        _lru_step_kernel,
        out_shape=(jax.ShapeDtypeStruct((n, D_LRU), bf16), jax.ShapeDtypeStruct((n, D_LRU), f32)),
        name="lru_step",
    )(xl, gl, s0, s1, s2, h0, w_conv, b_conv.reshape(1, D_LRU), wa, wx, ba.reshape(1, D_LRU),
      bx.reshape(1, D_LRU), lam.reshape(1, D_LRU), g_lru.reshape(1, D_LRU))


ATT_ROWS = 256


def _attn_kernel(q_ref, k_ref, v_ref, lft_ref, g_ref, o_ref, nf_ref, att_ref, m_ref, l_ref, *, blk):
    qi = pl.program_id(1)
    n_blk = nf_ref.shape[0]

    @pl.when(qi == 0)
    def _():
        r = lax.broadcasted_iota(jnp.int32, (blk, 2 * blk), 0)
        c = lax.broadcasted_iota(jnp.int32, (blk, 2 * blk), 1)
        tri_ones = jnp.where((r <= c) | (c >= blk), 1.0, 0.0).astype(bf16)
        carry = jnp.zeros((F_ROWS, blk), f32)
        for j in range(n_blk):
            hi, mid, lo = _split3(lft_ref[:, j * blk:(j + 1) * blk])
            res = (jnp.dot(hi, tri_ones, preferred_element_type=f32)
                   + jnp.dot(mid, tri_ones, preferred_element_type=f32)
                   + jnp.dot(lo, tri_ones, preferred_element_type=f32))
            cum = carry + res[:, :blk]
            nf_ref[j] = -cum[0:N_HEADS]
            carry = carry + res[:, blk:]

    rows = lax.broadcasted_iota(jnp.int32, (blk, blk), 0)
    colsi = lax.broadcasted_iota(jnp.int32, (blk, blk), 1)
    causal = colsi <= rows

    m_ref[...] = jnp.full(m_ref.shape, NEG_INF, f32)
    l_ref[...] = jnp.zeros(l_ref.shape, f32)
    att_ref[...] = jnp.zeros(att_ref.shape, f32)

    def kv_block(j, masked):
        off = pl.multiple_of(j * blk, blk)
        nf = nf_ref[j]
        for h, r0 in [(h, r0) for h in range(N_HEADS) for r0 in range(0, blk, ATT_ROWS)]:
            hs = slice(h * HEAD_DIM, (h + 1) * HEAD_DIM)
            rs = slice(r0, r0 + ATT_ROWS)
            s = lax.dot_general(q_ref[rs, hs], k_ref[pl.ds(off, blk), hs], (((1,), (1,)), ((), ())),
                                preferred_element_type=f32) + nf[h:h + 1, :]
            if masked:
                s = jnp.where(causal[rs], s, NEG_INF)
            m_old = m_ref[h, rs]
            m_new = jnp.maximum(m_old, jnp.max(s, axis=-1, keepdims=True))
            alpha = jnp.exp(m_old - m_new)
            p = jnp.exp(s - jnp.concatenate([m_new] * (blk // LANES), axis=1))
            m_ref[h, rs] = m_new
            l_ref[h, rs] = alpha * l_ref[h, rs] + jnp.sum(p, axis=-1, keepdims=True)
            att_ref[rs, hs] = alpha * att_ref[rs, hs] + jnp.dot(
                p.astype(bf16), v_ref[pl.ds(off, blk), hs], preferred_element_type=f32)

    def body(j, carry):
        kv_block(j, False)
        return carry

    lax.fori_loop(0, qi, body, 0)
    kv_block(qi, True)
    for h in range(N_HEADS):
        hs = slice(h * HEAD_DIM, (h + 1) * HEAD_DIM)
        att_ref[:, hs] = att_ref[:, hs] / l_ref[h]
    o_ref[...] = _rms(att_ref[...], g_ref[...]).astype(o_ref.dtype)


def _attn(q, kb, vb, lft, g_att, *, n_seq, seq, blk):
    nq = seq // blk
    return pl.pallas_call(
        functools.partial(_attn_kernel, blk=blk),
        grid=(n_seq, nq),
        in_specs=[
            pl.BlockSpec((blk, D_ATT), lambda b, i: (b * nq + i, 0)),
            pl.BlockSpec((seq, D_ATT), lambda b, i: (b, 0)),
            pl.BlockSpec((seq, D_ATT), lambda b, i: (b, 0)),
            pl.BlockSpec((F_ROWS, seq), lambda b, i: (0, b)),
            pl.BlockSpec((1, D_ATT), lambda b, i: (0, 0)),
        ],
        out_specs=pl.BlockSpec((blk, D_ATT), lambda b, i: (b * nq + i, 0)),
        out_shape=jax.ShapeDtypeStruct((n_seq * seq, D_ATT), bf16),
        scratch_shapes=[pltpu.VMEM((nq, N_HEADS, blk), f32), pltpu.VMEM((blk, D_ATT), f32),
                        pltpu.VMEM((N_HEADS, blk, LANES), f32), pltpu.VMEM((N_HEADS, blk, LANES), f32)],
        compiler_params=_params(("parallel", "arbitrary"), 40),
        name="attn",
    )(q, kb, vb, lft, g_att.reshape(1, D_ATT))


PAGES_PER_STEP = 2
PAGE_SLOTS = 2 * PAGES_PER_STEP
N_PARTIAL = 4


def _tree_sum(xs):
    while len(xs) > 1:
        xs = [a + b for a, b in zip(xs[0::2], xs[1::2])] + xs[len(xs) & ~1:]
    return xs[0]


def _decode_kernel(pt_ref, q_ref, kn_ref, vn_ref, lfn_ref, g_ref, k_hbm, v_hbm, lf_hbm,
                   o_ref, kbuf, vbuf, lfbuf, sc_ref, tri_ref, sem, *, n_pages):
    page = kbuf.shape[1]
    steps_per_seq = n_pages // PAGES_PER_STEP
    n_steps = o_ref.shape[0] * steps_per_seq

    def page_copies(slot, pid):
        return (pltpu.make_async_copy(k_hbm.at[pid], kbuf.at[slot], sem.at[0, slot]),
                pltpu.make_async_copy(v_hbm.at[pid], vbuf.at[slot], sem.at[1, slot]),
                pltpu.make_async_copy(lf_hbm.at[pid], lfbuf.at[slot], sem.at[2, slot]))

    def slot_base(step):
        return (step % 2) * PAGES_PER_STEP

    def start_fetch(step):
        for u in range(PAGES_PER_STEP):
            for cp in page_copies(slot_base(step) + u, pt_ref[step * PAGES_PER_STEP + u]):
                cp.start()

    def wait_fetch(step):
        for u in range(PAGES_PER_STEP):
            for cp in page_copies(slot_base(step) + u, 0):
                cp.wait()

    r = lax.broadcasted_iota(jnp.int32, (page, 2 * page), 0)
    c = lax.broadcasted_iota(jnp.int32, (page, 2 * page), 1)
    tri_ref[...] = jnp.where((r <= c) | (c >= page), 1.0, 0.0).astype(bf16)
    lane = lax.broadcasted_iota(jnp.int32, (N_HEADS, LANES), 1)

    start_fetch(0)

    def body(step, state):
        @pl.when(step + 1 < n_steps)
        def _():
            start_fetch(step + 1)

        wait_fetch(step)
        b = step // steps_per_seq
        j = step % steps_per_seq
        base = slot_base(step)
        first = j == 0
        m_run, l_run, a_run, car = state
        m_run = jnp.where(first, NEG_INF, m_run)
        l_run = jnp.where(first, 0.0, l_run)
        a_run = jnp.where(first, 0.0, a_run)
        car = jnp.where(first, 0.0, car)
        q = q_ref[b]

        lf2 = jnp.concatenate([lfbuf[base + u] for u in range(PAGES_PER_STEP)], axis=0)
        hi, mid, lo = _split3(lf2)
        tri_ones = tri_ref[...]
        res = (jnp.dot(hi, tri_ones, preferred_element_type=f32)
               + jnp.dot(mid, tri_ones, preferred_element_type=f32)
               + jnp.dot(lo, tri_ones, preferred_element_type=f32))

        cars, m_loc = [], []
        for u in range(PAGES_PER_STEP):
            rows = slice(u * N_HEADS, (u + 1) * N_HEADS)
            cum = res[rows, :page]
            part = [None] * N_PARTIAL
            for s in range(page):
                t = kbuf[base + u, s] * q - jnp.where(lane == s, cum, 0.0)
                sc = jnp.broadcast_to(jnp.sum(t, axis=-1, keepdims=True), (N_HEADS, LANES))
                sc_ref[u, s] = sc
                i = s % N_PARTIAL
                part[i] = sc if part[i] is None else jnp.maximum(part[i], sc)
            cars.append(car)
            m_loc.append(functools.reduce(jnp.maximum, part) - car)
            car = car + res[rows, page:]

        m_new = functools.reduce(jnp.maximum, m_loc + [m_run])
        alpha = jnp.exp(m_run - m_new)
        l_parts, a_parts = [l_run * alpha], [a_run * alpha]
        for u in range(PAGES_PER_STEP):
            shift = m_new + cars[u]
            lp = [None] * N_PARTIAL
            ap = [None] * N_PARTIAL
            for s in range(page):
                pr = jnp.exp(sc_ref[u, s] - shift)
                pv = pr * vbuf[base + u, s]
                i = s % N_PARTIAL
                lp[i] = pr if lp[i] is None else lp[i] + pr
                ap[i] = pv if ap[i] is None else ap[i] + pv
            l_parts += lp
            a_parts += ap
        l_run = _tree_sum(l_parts)
        a_run = _tree_sum(a_parts)

        @pl.when(j == steps_per_seq - 1)
        def _():
            s_new = jnp.sum(kn_ref[b] * q, axis=-1, keepdims=True) - (car + lfn_ref[b])
            m_all = jnp.maximum(m_new, s_new)
            w = jnp.exp(m_new - m_all)
            w_new = jnp.exp(s_new - m_all)
            o = (a_run * w + w_new * vn_ref[b]) / (l_run * w + w_new)
            ms = jnp.sum(jnp.sum(o * o, axis=-1, keepdims=True), axis=0, keepdims=True) / D_ATT
            o_ref[b] = o * lax.rsqrt(ms + EPS) * g_ref[...]

        return m_new, l_run, a_run, car

    zero = jnp.zeros((N_HEADS, LANES), f32)
    lax.fori_loop(0, n_steps, body, (zero, zero, zero, zero))


def _decode(page_table, q, k_new, v_new, lf_new, cache_k, cache_v, cache_lf_t, g_att):
    n_seq, n_pages = page_table.shape
    page = cache_k.shape[1]
    assert n_pages % PAGES_PER_STEP == 0 and page == LANES
    pt = page_table.reshape(-1)
    tok = pl.BlockSpec((n_seq, N_HEADS, HEAD_DIM), lambda i, pt: (0, 0, 0))
    hbm = pl.BlockSpec(memory_space=pl.ANY)
    grid_spec = pltpu.PrefetchScalarGridSpec(
        num_scalar_prefetch=1,
        grid=(1,),
        in_specs=[tok, tok, tok, tok, pl.BlockSpec((N_HEADS, HEAD_DIM), lambda i, pt: (0, 0)),
                  hbm, hbm, hbm],
        out_specs=tok,
        scratch_shapes=[pltpu.VMEM((PAGE_SLOTS, page, N_HEADS, HEAD_DIM), f32),
                        pltpu.VMEM((PAGE_SLOTS, page, N_HEADS, HEAD_DIM), f32),
                        pltpu.VMEM((PAGE_SLOTS, N_HEADS, page), f32),
                        pltpu.VMEM((PAGES_PER_STEP, page, N_HEADS, LANES), f32),
                        pltpu.VMEM((page, 2 * page), bf16),
                        pltpu.SemaphoreType.DMA((3, PAGE_SLOTS))],
    )
    return pl.pallas_call(
        functools.partial(_decode_kernel, n_pages=n_pages),
        grid_spec=grid_spec,
        out_shape=jax.ShapeDtypeStruct((n_seq, N_HEADS, HEAD_DIM), f32),
        compiler_params=_params(("arbitrary",), 24),
        name="decode",
    )(pt, q, k_new, v_new, lf_new, g_att.reshape(N_HEADS, HEAD_DIM), cache_k, cache_v, cache_lf_t)


def _out_proj_kernel(att_ref, lru_ref, x_ref, w_ref, g_ref, x1_ref, xn_ref):
    mix = jnp.concatenate([att_ref[...].astype(bf16), lru_ref[...].astype(bf16)], axis=1)
    x1 = x_ref[...] + jnp.dot(mix, w_ref[...], preferred_element_type=f32)
    x1_ref[...] = x1
    xn_ref[...] = _rms(x1, g_ref[...]).astype(bf16)


def _out_proj(att_n, lru_n, x, w_out, g_ffn, *, tm):
    m = x.shape[0]
    row = lambda i: (i, 0)
    const = lambda i: (0, 0)
    return pl.pallas_call(
        _out_proj_kernel,
        grid=(m // tm,),
        in_specs=[pl.BlockSpec((tm, D_ATT), row), pl.BlockSpec((tm, D_LRU), row),
                  pl.BlockSpec((tm, D_MODEL), row),
                  pl.BlockSpec((D_MODEL, D_MODEL), const, pipeline_mode=pl.Buffered(1)),
                  pl.BlockSpec((1, D_MODEL), const)],
        out_specs=(pl.BlockSpec((tm, D_MODEL), row), pl.BlockSpec((tm, D_MODEL), row)),
        out_shape=(jax.ShapeDtypeStruct((m, D_MODEL), f32), jax.ShapeDtypeStruct((m, D_MODEL), bf16)),
        compiler_params=_params(("parallel",), 48),
        name="out_proj",
    )(att_n, lru_n, x, w_out, g_ffn.reshape(1, D_MODEL))


def _ffn_kernel(xn_ref, x1_ref, wu_ref, wd_ref, g_ref, o_ref, acc_ref):
    f = pl.program_id(1)

    @pl.when(f == 0)
    def _():
        acc_ref[...] = jnp.zeros(acc_ref.shape, f32)

    hf = jnp.dot(xn_ref[...], wu_ref[...], preferred_element_type=f32)
    act = jnp.square(jnp.maximum(hf, 0.0)).astype(bf16)
    acc_ref[...] += jnp.dot(act, wd_ref[...], preferred_element_type=f32)

    @pl.when(f == pl.num_programs(1) - 1)
    def _():
        o_ref[...] = _rms(x1_ref[...] + acc_ref[...], g_ref[...])


def _ffn(xn, x1, w_up, w_down, g_final, *, tm, tf):
    m = x1.shape[0]
    row = lambda i, f: (i, 0)
    return pl.pallas_call(
        _ffn_kernel,
        grid=(m // tm, D_FF // tf),
        in_specs=[pl.BlockSpec((tm, D_MODEL), row, pipeline_mode=pl.Buffered(1)),
                  pl.BlockSpec((tm, D_MODEL), row, pipeline_mode=pl.Buffered(1)),
                  pl.BlockSpec((D_MODEL, tf), lambda i, f: (0, f)),
                  pl.BlockSpec((tf, D_MODEL), lambda i, f: (f, 0)),
                  pl.BlockSpec((1, D_MODEL), lambda i, f: (0, 0))],
        out_specs=pl.BlockSpec((tm, D_MODEL), row),
        out_shape=jax.ShapeDtypeStruct((m, D_MODEL), f32),
        scratch_shapes=[pltpu.VMEM((tm, D_MODEL), f32)],
        compiler_params=_params(("parallel", "arbitrary"), 56),
        name="ffn",
    )(xn, x1, w_up, w_down, g_final.reshape(1, D_MODEL))


def kernel(x_prompt, x_sample, cache_k, cache_v, cache_logf, state_conv, state_h, page_table,
           g_mix, w_in, b_f, w_conv, b_conv, w_gate_a, b_gate_a, w_gate_x, b_gate_x,
           lru_lambda, g_att_out, g_lru_out, w_out, g_ffn, w_up, w_down, g_final):
    depth = w_in.shape[0]
    n_seq, seq, _ = x_prompt.shape
    n_dec = x_sample.shape[0]
    assert depth == 1 and x_sample.shape[1] == 1

    l = 0
    c_f = 3 * D_ATT
    w_qkv = w_in[l][:, :c_f].astype(bf16)
    w_lru = w_in[l][:, c_f + N_HEADS:].astype(bf16)
    wft = jnp.pad(w_in[l][:, c_f:c_f + N_HEADS].T, ((0, F_ROWS - N_HEADS), (0, 0))).astype(bf16)
    wa = w_gate_a[l].astype(bf16)
    wx = w_gate_x[l].astype(bf16)
    wo = w_out[l].astype(bf16)
    wu = w_up[l].astype(bf16)
    wd = w_down[l].astype(bf16)
    lru_w = (w_conv[l], b_conv[l], wa, wx, b_gate_a[l], b_gate_x[l], lru_lambda[l], g_lru_out[l])

    xp = x_prompt.reshape(n_seq * seq, D_MODEL)
    q, k, v, kb, vb, xl, gl, lft = _in_proj(xp, g_mix[l], w_qkv, w_lru, wft, b_f[l], tm=256, q_dtype=bf16)
    lru_n, h_t = _lru(xl, gl, *lru_w, n_seq=n_seq, seq=seq, tb=256)
    att_n = _attn(q, kb, vb, lft, g_att_out[l], n_seq=n_seq, seq=seq, blk=256)
    x1, xn = _out_proj(att_n, lru_n, xp, wo, g_ffn[l], tm=512)
    y_prompt = _ffn(xn, x1, wu, wd, g_final, tm=1024, tf=512).reshape(n_seq, seq, D_MODEL)

    new_k_p = k.reshape(1, n_seq, seq, N_HEADS, HEAD_DIM)
    new_v_p = v.reshape(1, n_seq, seq, N_HEADS, HEAD_DIM)
    new_f_p = lft[:N_HEADS].T.reshape(1, n_seq, seq, N_HEADS)
    new_c_p = xl.reshape(n_seq, seq, D_LRU)[:, seq - (CONV_W - 1):, :][None]
    new_h_p = h_t.reshape(1, n_seq, D_LRU)

    xs = x_sample.reshape(n_dec, D_MODEL)
    qs, ks, vs, _, _, xls, gls, lfts = _in_proj(xs, g_mix[l], w_qkv, w_lru, wft, b_f[l], tm=n_dec,
                                                q_dtype=f32)
    lf_s = lfts[:N_HEADS].T
    heads = lambda t: t.reshape(n_dec, N_HEADS, HEAD_DIM)
    lf_wide = jnp.broadcast_to(lf_s[:, :, None], (n_dec, N_HEADS, HEAD_DIM))
    att_s = _decode(page_table, heads(qs), heads(ks), heads(vs), lf_wide, cache_k[l], cache_v[l],
                    jnp.swapaxes(cache_logf[l], 1, 2), g_att_out[l]).reshape(n_dec, D_ATT)
    sc = state_conv[l]
    lru_s, h_s = _lru_step(xls, gls, sc[:, 0], sc[:, 1], sc[:, 2], state_h[l], *lru_w)
    x1s, xns = _out_proj(att_s, lru_s, xs, wo, g_ffn[l], tm=n_dec)
    y_sample = _ffn(xns, x1s, wu, wd, g_final, tm=n_dec, tf=512).reshape(n_dec, 1, D_MODEL)

    new_k_s = ks.reshape(1, n_dec, 1, N_HEADS, HEAD_DIM)
    new_v_s = vs.reshape(1, n_dec, 1, N_HEADS, HEAD_DIM)
    new_f_s = lf_s.reshape(1, n_dec, 1, N_HEADS)
    new_c_s = jnp.stack([sc[:, 1], sc[:, 2], xls], axis=1)[None]
    new_h_s = h_s[None]

    return (y_prompt, y_sample, new_k_p, new_v_p, new_f_p, new_c_p, new_h_p,
            new_k_s, new_v_s, new_f_s, new_c_s, new_h_s)
```

```python
import functools

import jax
import jax.numpy as jnp
from jax import lax
from jax.experimental import pallas as pl
from jax.experimental.pallas import tpu as pltpu

D_MODEL = 2048
D_ATT = 1024
D_LRU = 1024
HEAD_DIM = 128
N_HEADS = 8
N_LRU_BLOCKS = 8
LRU_BLOCK = 128
CONV_W = 4
LRU_C = 8.0
D_FF = 4 * D_MODEL
EPS = 1e-6
NEG_INF = -1e30
F_ROWS = 16
SUBLANES = 8
LANES = 128
MIB = 1024 * 1024

f32 = jnp.float32
bf16 = jnp.bfloat16


def _rms(x, g):
    return x * lax.rsqrt(jnp.mean(x * x, axis=-1, keepdims=True) + EPS) * g


def _softplus(x):
    return jnp.maximum(x, 0.0) + jnp.log1p(jnp.exp(-jnp.abs(x)))


def _sigmoid(x):
    return 0.5 * jnp.tanh(0.5 * x) + 0.5


def _gelu_tanh(x):
    c = 0.7978845608028654
    return x * (0.5 * (1.0 + jnp.tanh(c * (x + 0.044715 * (x * x * x)))))


def _split3(x):
    hi = x.astype(bf16)
    r1 = x - hi.astype(f32)
    mid = r1.astype(bf16)
    lo = (r1 - mid.astype(f32)).astype(bf16)
    return hi, mid, lo


def _params(sem, vmem_mib, flags=None):
    return pltpu.CompilerParams(dimension_semantics=sem, vmem_limit_bytes=vmem_mib * MIB, flags=flags)


def _in_proj_kernel(x_ref, g_ref, wqkv_ref, wlru_ref, wft_ref, bf_ref,
                    q_ref, k_ref, v_ref, kb_ref, vb_ref, xl_ref, gl_ref, lft_ref):
    xn = _rms(x_ref[...], g_ref[...]).astype(bf16)

    def cols(w_ref, c):
        return jnp.dot(xn, w_ref[:, c * D_ATT:(c + 1) * D_ATT], preferred_element_type=f32)

    q_ref[...] = (cols(wqkv_ref, 0) * (HEAD_DIM ** -0.5)).astype(q_ref.dtype)
    k = cols(wqkv_ref, 1)
    k_ref[...] = k
    kb_ref[...] = k.astype(bf16)
    v = cols(wqkv_ref, 2)
    v_ref[...] = v
    vb_ref[...] = v.astype(bf16)
    xl_ref[...] = cols(wlru_ref, 0)
    gl_ref[...] = cols(wlru_ref, 1)
    t = lax.dot_general(wft_ref[...], xn, (((1,), (1,)), ((), ())),
                        preferred_element_type=f32) + bf_ref[...]
    lft_ref[...] = jnp.minimum(t, 0.0) - jnp.log1p(jnp.exp(-jnp.abs(t)))


def _in_proj(x, g_mix, w_qkv, w_lru, wft, b_f, *, tm, q_dtype):
    m = x.shape[0]
    bfb = jnp.broadcast_to(jnp.pad(b_f, (0, F_ROWS - N_HEADS))[:, None], (F_ROWS, tm)).astype(f32)
    row = lambda i: (i, 0)
    const = lambda i: (0, 0)
    wide = pl.BlockSpec((tm, D_ATT), row)
    out_shape = (
        jax.ShapeDtypeStruct((m, D_ATT), q_dtype),
        jax.ShapeDtypeStruct((m, D_ATT), f32), jax.ShapeDtypeStruct((m, D_ATT), f32),
        jax.ShapeDtypeStruct((m, D_ATT), bf16), jax.ShapeDtypeStruct((m, D_ATT), bf16),
        jax.ShapeDtypeStruct((m, D_LRU), f32), jax.ShapeDtypeStruct((m, D_LRU), f32),
        jax.ShapeDtypeStruct((F_ROWS, m), f32),
    )
    return pl.pallas_call(
        _in_proj_kernel,
        grid=(m // tm,),
        in_specs=[
            pl.BlockSpec((tm, D_MODEL), row),
            pl.BlockSpec((1, D_MODEL), const),
            pl.BlockSpec((D_MODEL, 3 * D_ATT), const, pipeline_mode=pl.Buffered(1)),
            pl.BlockSpec((D_MODEL, 2 * D_LRU), const, pipeline_mode=pl.Buffered(1)),
            pl.BlockSpec((F_ROWS, D_MODEL), const),
            pl.BlockSpec((F_ROWS, tm), const),
        ],
        out_specs=(wide,) * 7 + (pl.BlockSpec((F_ROWS, tm), lambda i: (0, i)),),
        out_shape=out_shape,
        compiler_params=_params(("parallel",), 52),
        name="in_proj",
    )(x, g_mix.reshape(1, D_MODEL), w_qkv, w_lru, wft, bfb)


def _lru_gates(xc, wa_ref, wx_ref, ba, bx, lam):
    xcb = xc.astype(bf16)
    r_parts, i_parts = [], []
    for n in range(N_LRU_BLOCKS):
        blk = xcb[:, n * LRU_BLOCK:(n + 1) * LRU_BLOCK]
        r_parts.append(jnp.dot(blk, wa_ref[n], preferred_element_type=f32))
        i_parts.append(jnp.dot(blk, wx_ref[n], preferred_element_type=f32))
    r = _sigmoid(jnp.concatenate(r_parts, axis=1) + ba)
    i = _sigmoid(jnp.concatenate(i_parts, axis=1) + bx)
    log_a = (-LRU_C * r) * _softplus(-lam)
    a = jnp.exp(log_a)
    th = jnp.tanh(log_a)
    b = jnp.sqrt(-2.0 * th) * lax.rsqrt(1.0 - th) * i * xc
    return a, b


def _lru_kernel(xl_ref, gl_ref, wc_ref, bc_ref, wa_ref, wx_ref, ba_ref, bx_ref, lam_ref, g_ref,
                out_ref, ht_ref, ext_ref, h_ref, hs_ref):
    c = pl.program_id(1)
    tb = xl_ref.shape[0]

    @pl.when(c == 0)
    def _():
        ext_ref[0:SUBLANES, :] = jnp.zeros((SUBLANES, D_LRU), f32)
        h_ref[...] = jnp.zeros((SUBLANES, D_LRU), f32)

    @pl.when(c > 0)
    def _():
        ext_ref[0:SUBLANES, :] = ext_ref[tb:tb + SUBLANES, :]

    x = xl_ref[...]
    ext_ref[SUBLANES:tb + SUBLANES, :] = x
    wc = wc_ref[...]
    xc = bc_ref[...] + wc[0:1] * ext_ref[5:5 + tb, :]
    xc = xc + wc[1:2] * ext_ref[6:6 + tb, :]
    xc = xc + wc[2:3] * ext_ref[7:7 + tb, :]
    xc = xc + wc[3:4] * x

    a, b = _lru_gates(xc, wa_ref, wx_ref, ba_ref[...], bx_ref[...], lam_ref[...])

    row = lax.broadcasted_iota(jnp.int32, (tb, D_LRU), 0) & (SUBLANES - 1)
    for s in (1, 2, 4):
        a_sh = pltpu.roll(a, s, 0)
        b_sh = pltpu.roll(b, s, 0)
        take = row >= s
        b = jnp.where(take, a * b_sh + b, b)
        a = jnp.where(take, a * a_sh, a)

    carry = h_ref[...]
    for g in range(tb // SUBLANES):
        sl = slice(g * SUBLANES, (g + 1) * SUBLANES)
        hg = a[sl] * carry + b[sl]
        hs_ref[sl, :] = hg
        carry = jnp.broadcast_to(hg[SUBLANES - 1:SUBLANES, :], (SUBLANES, D_LRU))
    h_ref[...] = carry

    lru = _gelu_tanh(gl_ref[...]) * hs_ref[...]
    out_ref[...] = _rms(lru, g_ref[...]).astype(out_ref.dtype)

    @pl.when(c == pl.num_programs(1) - 1)
    def _():
        ht_ref[0] = carry[0:1]


def _lru(xl, gl, w_conv, b_conv, wa, wx, ba, bx, lam, g_lru, *, n_seq, seq, tb):
    nc = seq // tb
    row = lambda b, c: (b * nc + c, 0)
    const2 = lambda b, c: (0, 0)
    const3 = lambda b, c: (0, 0, 0)
    vec = pl.BlockSpec((1, D_LRU), const2)
    return pl.pallas_call(
        _lru_kernel,
        grid=(n_seq, nc),
        in_specs=[
            pl.BlockSpec((tb, D_LRU), row), pl.BlockSpec((tb, D_LRU), row),
            pl.BlockSpec((CONV_W, D_LRU), const2), vec,
            pl.BlockSpec((N_LRU_BLOCKS, LRU_BLOCK, LRU_BLOCK), const3),
            pl.BlockSpec((N_LRU_BLOCKS, LRU_BLOCK, LRU_BLOCK), const3),
            vec, vec, vec, vec,
        ],
        out_specs=(pl.BlockSpec((tb, D_LRU), row),
                   pl.BlockSpec((1, 1, D_LRU), lambda b, c: (b, 0, 0))),
        out_shape=(jax.ShapeDtypeStruct((n_seq * seq, D_LRU), bf16),
                   jax.ShapeDtypeStruct((n_seq, 1, D_LRU), f32)),
        scratch_shapes=[pltpu.VMEM((tb + SUBLANES, D_LRU), f32),
                        pltpu.VMEM((SUBLANES, D_LRU), f32),
                        pltpu.VMEM((tb, D_LRU), f32)],
        compiler_params=_params(("parallel", "arbitrary"), 40),
        name="lru",
    )(xl, gl, w_conv, b_conv.reshape(1, D_LRU), wa, wx, ba.reshape(1, D_LRU),
      bx.reshape(1, D_LRU), lam.reshape(1, D_LRU), g_lru.reshape(1, D_LRU))


def _lru_step_kernel(xl_ref, gl_ref, s0_ref, s1_ref, s2_ref, h0_ref, wc_ref, bc_ref, wa_ref, wx_ref,
                     ba_ref, bx_ref, lam_ref, g_ref, out_ref, h_ref):
    wc = wc_ref[...]
    xc = bc_ref[...] + wc[0:1] * s0_ref[...]
    xc = xc + wc[1:2] * s1_ref[...]
    xc = xc + wc[2:3] * s2_ref[...]
    xc = xc + wc[3:4] * xl_ref[...]
    a, b = _lru_gates(xc, wa_ref, wx_ref, ba_ref[...], bx_ref[...], lam_ref[...])
    h = a * h0_ref[...] + b
    h_ref[...] = h
    out_ref[...] = _rms(_gelu_tanh(gl_ref[...]) * h, g_ref[...]).astype(out_ref.dtype)


def _lru_step(xl, gl, s0, s1, s2, h0, w_conv, b_conv, wa, wx, ba, bx, lam, g_lru):
    n = xl.shape[0]
    return pl.pallas_call(
        _lru_step_kernel,
        out_shape=(jax.ShapeDtypeStruct((n, D_LRU), bf16), jax.ShapeDtypeStruct((n, D_LRU), f32)),
        name="lru_step",
    )(xl, gl, s0, s1, s2, h0, w_conv, b_conv.reshape(1, D_LRU), wa, wx, ba.reshape(1, D_LRU),
      bx.reshape(1, D_LRU), lam.reshape(1, D_LRU), g_lru.reshape(1, D_LRU))


ATT_ROWS = 256


def _attn_kernel(q_ref, k_ref, v_ref, lft_ref, g_ref, o_ref, nf_ref, att_ref, m_ref, l_ref, *, blk):
    qi = pl.program_id(1)
    n_blk = nf_ref.shape[0]

    @pl.when(qi == 0)
    def _():
        r = lax.broadcasted_iota(jnp.int32, (blk, 2 * blk), 0)
        c = lax.broadcasted_iota(jnp.int32, (blk, 2 * blk), 1)
        tri_ones = jnp.where((r <= c) | (c >= blk), 1.0, 0.0).astype(bf16)
        carry = jnp.zeros((F_ROWS, blk), f32)
        for j in range(n_blk):
            hi, mid, lo = _split3(lft_ref[:, j * blk:(j + 1) * blk])
            res = (jnp.dot(hi, tri_ones, preferred_element_type=f32)
                   + jnp.dot(mid, tri_ones, preferred_element_type=f32)
                   + jnp.dot(lo, tri_ones, preferred_element_type=f32))
            cum = carry + res[:, :blk]
            nf_ref[j] = -cum[0:N_HEADS]
            carry = carry + res[:, blk:]

    rows = lax.broadcasted_iota(jnp.int32, (blk, blk), 0)
    colsi = lax.broadcasted_iota(jnp.int32, (blk, blk), 1)
    causal = colsi <= rows

    m_ref[...] = jnp.full(m_ref.shape, NEG_INF, f32)
    l_ref[...] = jnp.zeros(l_ref.shape, f32)
    att_ref[...] = jnp.zeros(att_ref.shape, f32)

    def kv_block(j, masked):
        off = pl.multiple_of(j * blk, blk)
        nf = nf_ref[j]
        for h, r0 in [(h, r0) for h in range(N_HEADS) for r0 in range(0, blk, ATT_ROWS)]:
            hs = slice(h * HEAD_DIM, (h + 1) * HEAD_DIM)
            rs = slice(r0, r0 + ATT_ROWS)
            s = lax.dot_general(q_ref[rs, hs], k_ref[pl.ds(off, blk), hs], (((1,), (1,)), ((), ())),
                                preferred_element_type=f32) + nf[h:h + 1, :]
            if masked:
                s = jnp.where(causal[rs], s, NEG_INF)
            m_old = m_ref[h, rs]
            m_new = jnp.maximum(m_old, jnp.max(s, axis=-1, keepdims=True))
            alpha = jnp.exp(m_old - m_new)
            p = jnp.exp(s - jnp.concatenate([m_new] * (blk // LANES), axis=1))
            m_ref[h, rs] = m_new
            l_ref[h, rs] = alpha * l_ref[h, rs] + jnp.sum(p, axis=-1, keepdims=True)
            att_ref[rs, hs] = alpha * att_ref[rs, hs] + jnp.dot(
                p.astype(bf16), v_ref[pl.ds(off, blk), hs], preferred_element_type=f32)

    def body(j, carry):
        kv_block(j, False)
        return carry

    lax.fori_loop(0, qi, body, 0)
    kv_block(qi, True)
    for h in range(N_HEADS):
        hs = slice(h * HEAD_DIM, (h + 1) * HEAD_DIM)
        att_ref[:, hs] = att_ref[:, hs] / l_ref[h]
    o_ref[...] = _rms(att_ref[...], g_ref[...]).astype(o_ref.dtype)


def _attn(q, kb, vb, lft, g_att, *, n_seq, seq, blk):
    nq = seq // blk
    return pl.pallas_call(
        functools.partial(_attn_kernel, blk=blk),
        grid=(n_seq, nq),
        in_specs=[
            pl.BlockSpec((blk, D_ATT), lambda b, i: (b * nq + i, 0)),
            pl.BlockSpec((seq, D_ATT), lambda b, i: (b, 0)),
            pl.BlockSpec((seq, D_ATT), lambda b, i: (b, 0)),
            pl.BlockSpec((F_ROWS, seq), lambda b, i: (0, b)),
            pl.BlockSpec((1, D_ATT), lambda b, i: (0, 0)),
        ],
        out_specs=pl.BlockSpec((blk, D_ATT), lambda b, i: (b * nq + i, 0)),
        out_shape=jax.ShapeDtypeStruct((n_seq * seq, D_ATT), bf16),
        scratch_shapes=[pltpu.VMEM((nq, N_HEADS, blk), f32), pltpu.VMEM((blk, D_ATT), f32),
                        pltpu.VMEM((N_HEADS, blk, LANES), f32), pltpu.VMEM((N_HEADS, blk, LANES), f32)],
        compiler_params=_params(("parallel", "arbitrary"), 40),
        name="attn",
    )(q, kb, vb, lft, g_att.reshape(1, D_ATT))


PAGES_PER_STEP = 2
RING_STEPS = 8
FETCH_AHEAD = RING_STEPS - 2
PAGE_SLOTS = RING_STEPS * PAGES_PER_STEP
N_PARTIAL = 4


def _tree_sum(xs):
    while len(xs) > 1:
        xs = [a + b for a, b in zip(xs[0::2], xs[1::2])] + xs[len(xs) & ~1:]
    return xs[0]


def _alternate(*passes):
    results = [None] * len(passes)
    live = list(range(len(passes)))
    while live:
        for i in list(live):
            try:
                next(passes[i])
            except StopIteration as done:
                results[i] = done.value
                live.remove(i)
    return results


def _decode_kernel(pt_ref, q_ref, kn_ref, vn_ref, lfn_ref, g_ref, k_hbm, v_hbm, lf_hbm,
                   o_ref, kbuf, vbuf, lfbuf, sc_a, sc_b, tri_ref, sem, *, n_pages):
    page = kbuf.shape[1]
    steps_per_seq = n_pages // PAGES_PER_STEP
    n_steps = o_ref.shape[0] * steps_per_seq
    last_of_seq = steps_per_seq - 1

    def page_copies(slot, pid):
        return (pltpu.make_async_copy(k_hbm.at[pid], kbuf.at[slot], sem.at[0, slot]),
                pltpu.make_async_copy(v_hbm.at[pid], vbuf.at[slot], sem.at[1, slot]),
                pltpu.make_async_copy(lf_hbm.at[pid], lfbuf.at[slot], sem.at[2, slot]))

    def slot_base(step):
        return (step % RING_STEPS) * PAGES_PER_STEP

    def start_fetch(step):
        for u in range(PAGES_PER_STEP):
            for cp in page_copies(slot_base(step) + u, pt_ref[step * PAGES_PER_STEP + u]):
                cp.start()

    def wait_fetch(step):
        for u in range(PAGES_PER_STEP):
            for cp in page_copies(slot_base(step) + u, 0):
                cp.wait()

    r = lax.broadcasted_iota(jnp.int32, (page, 2 * page), 0)
    c = lax.broadcasted_iota(jnp.int32, (page, 2 * page), 1)
    tri_ref[...] = jnp.where((r <= c) | (c >= page), 1.0, 0.0).astype(bf16)
    lane = lax.broadcasted_iota(jnp.int32, (N_HEADS, LANES), 1)

    def score_pass(step, ring, car, sc_ref):
        q = q_ref[step // steps_per_seq]
        base = ring * PAGES_PER_STEP
        car = jnp.where(step % steps_per_seq == 0, 0.0, car)
        lf2 = jnp.concatenate([lfbuf[base + u] for u in range(PAGES_PER_STEP)], axis=0)
        hi, mid, lo = _split3(lf2)
        tri_ones = tri_ref[...]
        res = (jnp.dot(hi, tri_ones, preferred_element_type=f32)
               + jnp.dot(mid, tri_ones, preferred_element_type=f32)
               + jnp.dot(lo, tri_ones, preferred_element_type=f32))
        cars, m_loc = [], []
        for u in range(PAGES_PER_STEP):
            rows = slice(u * N_HEADS, (u + 1) * N_HEADS)
            cum = res[rows, :page]
            part = [None] * N_PARTIAL
            for s in range(page):
                t = kbuf[base + u, s] * q - jnp.where(lane == s, cum, 0.0)
                sc = jnp.broadcast_to(jnp.sum(t, axis=-1, keepdims=True), (N_HEADS, LANES))
                sc_ref[u, s] = sc
                i = s % N_PARTIAL
                part[i] = sc if part[i] is None else jnp.maximum(part[i], sc)
                yield
            cars.append(car)
            m_loc.append(functools.reduce(jnp.maximum, part) - car)
            car = car + res[rows, page:]
        return tuple(m_loc), tuple(cars), car

    def value_pass(step, ring, soft, scored, sc_ref):
        m_loc, cars, _ = scored
        base = ring * PAGES_PER_STEP
        first = step % steps_per_seq == 0
        m_run = jnp.where(first, NEG_INF, soft[0])
        l_run = jnp.where(first, 0.0, soft[1])
        a_run = jnp.where(first, 0.0, soft[2])
        m_new = functools.reduce(jnp.maximum, list(m_loc) + [m_run])
        alpha = jnp.exp(m_run - m_new)
        l_parts, a_parts = [l_run * alpha], [a_run * alpha]
        for u in range(PAGES_PER_STEP):
            shift = m_new + cars[u]
            lp = [None] * N_PARTIAL
            ap = [None] * N_PARTIAL
            for s in range(page):
                pr = jnp.exp(sc_ref[u, s] - shift)
                pv = pr * vbuf[base + u, s]
                i = s % N_PARTIAL
                lp[i] = pr if lp[i] is None else lp[i] + pr
                ap[i] = pv if ap[i] is None else ap[i] + pv
                yield
            l_parts += lp
            a_parts += ap
        return m_new, _tree_sum(l_parts), _tree_sum(a_parts)

    def write_output(step, soft, car_end):
        b = step // steps_per_seq
        m_run, l_run, a_run = soft

        @pl.when(step % steps_per_seq == last_of_seq)
        def _():
            q = q_ref[b]
            s_new = jnp.sum(kn_ref[b] * q, axis=-1, keepdims=True) - (car_end + lfn_ref[b])
            m_all = jnp.maximum(m_run, s_new)
            w = jnp.exp(m_run - m_all)
            w_new = jnp.exp(s_new - m_all)
            o = (a_run * w + w_new * vn_ref[b]) / (l_run * w + w_new)
            ms = jnp.sum(jnp.sum(o * o, axis=-1, keepdims=True), axis=0, keepdims=True) / D_ATT
            o_ref[b] = o * lax.rsqrt(ms + EPS) * g_ref[...]

    def half(step, ring, soft, scored):
        nxt = step + 1
        sc_cur, sc_nxt = (sc_a, sc_b) if ring % 2 == 0 else (sc_b, sc_a)

        @pl.when(nxt + FETCH_AHEAD < n_steps)
        def _():
            start_fetch(nxt + FETCH_AHEAD)

        @pl.when(nxt < n_steps)
        def _():
            wait_fetch(nxt)

        soft, scored_nxt = _alternate(
            value_pass(step, ring, soft, scored, sc_cur),
            score_pass(jnp.minimum(nxt, n_steps - 1), (ring + 1) % RING_STEPS, scored[2], sc_nxt))
        write_output(step, soft, scored[2])
        return soft, scored_nxt

    for step in range(min(FETCH_AHEAD + 1, n_steps)):
        start_fetch(step)
    wait_fetch(0)
    zero = jnp.zeros((N_HEADS, LANES), f32)
    scored0, = _alternate(score_pass(0, 0, zero, sc_a))

    def trip(t, carry):
        soft, scored = carry
        for ring in range(RING_STEPS):
            soft, scored = half(t * RING_STEPS + ring, ring, soft, scored)
        return soft, scored

    lax.fori_loop(0, n_steps // RING_STEPS, trip, ((zero, zero, zero), scored0))


def _decode(page_table, q, k_new, v_new, lf_new, cache_k, cache_v, cache_lf_t, g_att):
    n_seq, n_pages = page_table.shape
    page = cache_k.shape[1]
    assert n_pages % PAGES_PER_STEP == 0 and page == LANES
    assert (n_seq * n_pages // PAGES_PER_STEP) % RING_STEPS == 0
    pt = page_table.reshape(-1)
    tok = pl.BlockSpec((n_seq, N_HEADS, HEAD_DIM), lambda i, pt: (0, 0, 0))
    hbm = pl.BlockSpec(memory_space=pl.ANY)
    grid_spec = pltpu.PrefetchScalarGridSpec(
        num_scalar_prefetch=1,
        grid=(1,),
        in_specs=[tok, tok, tok, tok, pl.BlockSpec((N_HEADS, HEAD_DIM), lambda i, pt: (0, 0)),
                  hbm, hbm, hbm],
        out_specs=tok,
        scratch_shapes=[pltpu.VMEM((PAGE_SLOTS, page, N_HEADS, HEAD_DIM), f32),
                        pltpu.VMEM((PAGE_SLOTS, page, N_HEADS, HEAD_DIM), f32),
                        pltpu.VMEM((PAGE_SLOTS, N_HEADS, page), f32),
                        pltpu.VMEM((PAGES_PER_STEP, page, N_HEADS, LANES), f32),
                        pltpu.VMEM((PAGES_PER_STEP, page, N_HEADS, LANES), f32),
                        pltpu.VMEM((page, 2 * page), bf16),
                        pltpu.SemaphoreType.DMA((3, PAGE_SLOTS))],
    )
    return pl.pallas_call(
        functools.partial(_decode_kernel, n_pages=n_pages),
        grid_spec=grid_spec,
        out_shape=jax.ShapeDtypeStruct((n_seq, N_HEADS, HEAD_DIM), f32),
        compiler_params=_params(("arbitrary",), 32),
        name="decode",
    )(pt, q, k_new, v_new, lf_new, g_att.reshape(N_HEADS, HEAD_DIM), cache_k, cache_v, cache_lf_t)


def _out_proj_kernel(att_ref, lru_ref, x_ref, w_ref, g_ref, x1_ref, xn_ref):
    mix = jnp.concatenate([att_ref[...].astype(bf16), lru_ref[...].astype(bf16)], axis=1)
    x1 = x_ref[...] + jnp.dot(mix, w_ref[...], preferred_element_type=f32)
    x1_ref[...] = x1
    xn_ref[...] = _rms(x1, g_ref[...]).astype(bf16)


def _out_proj(att_n, lru_n, x, w_out, g_ffn, *, tm):
    m = x.shape[0]
    row = lambda i: (i, 0)
    const = lambda i: (0, 0)
    return pl.pallas_call(
        _out_proj_kernel,
        grid=(m // tm,),
        in_specs=[pl.BlockSpec((tm, D_ATT), row), pl.BlockSpec((tm, D_LRU), row),
                  pl.BlockSpec((tm, D_MODEL), row),
                  pl.BlockSpec((D_MODEL, D_MODEL), const, pipeline_mode=pl.Buffered(1)),
                  pl.BlockSpec((1, D_MODEL), const)],
        out_specs=(pl.BlockSpec((tm, D_MODEL), row), pl.BlockSpec((tm, D_MODEL), row)),
        out_shape=(jax.ShapeDtypeStruct((m, D_MODEL), f32), jax.ShapeDtypeStruct((m, D_MODEL), bf16)),
        compiler_params=_params(("parallel",), 48),
        name="out_proj",
    )(att_n, lru_n, x, w_out, g_ffn.reshape(1, D_MODEL))


def _ffn_kernel(xn_ref, x1_ref, wu_ref, wd_ref, g_ref, o_ref):
    f = pl.program_id(1)

    @pl.when(f == 0)
    def _():
        o_ref[...] = x1_ref[...]

    hf = jnp.dot(xn_ref[...], wu_ref[...].astype(bf16), preferred_element_type=f32)
    act = jnp.square(jnp.maximum(hf, 0.0)).astype(bf16)
    o_ref[...] += jnp.dot(act, wd_ref[...].astype(bf16), preferred_element_type=f32)

    @pl.when(f == pl.num_programs(1) - 1)
    def _():
        o_ref[...] = _rms(o_ref[...], g_ref[...])


def _ffn(xn, x1, w_up, w_down, g_final, *, tm, tf):
    m = x1.shape[0]
    row = lambda i, f: (i, 0)
    return pl.pallas_call(
        _ffn_kernel,
        grid=(m // tm, D_FF // tf),
        in_specs=[pl.BlockSpec((tm, D_MODEL), row, pipeline_mode=pl.Buffered(1)),
                  pl.BlockSpec((tm, D_MODEL), row, pipeline_mode=pl.Buffered(1)),
                  pl.BlockSpec((D_MODEL, tf), lambda i, f: (0, f)),
                  pl.BlockSpec((tf, D_MODEL), lambda i, f: (f, 0)),
                  pl.BlockSpec((1, D_MODEL), lambda i, f: (0, 0))],
        out_specs=pl.BlockSpec((tm, D_MODEL), row),
        out_shape=jax.ShapeDtypeStruct((m, D_MODEL), f32),
        compiler_params=_params(("parallel", "arbitrary"), 56),
        name="ffn",
    )(xn, x1, w_up, w_down, g_final.reshape(1, D_MODEL))


def kernel(x_prompt, x_sample, cache_k, cache_v, cache_logf, state_conv, state_h, page_table,
           g_mix, w_in, b_f, w_conv, b_conv, w_gate_a, b_gate_a, w_gate_x, b_gate_x,
           lru_lambda, g_att_out, g_lru_out, w_out, g_ffn, w_up, w_down, g_final):
    depth = w_in.shape[0]
    n_seq, seq, _ = x_prompt.shape
    n_dec = x_sample.shape[0]
    assert depth == 1 and x_sample.shape[1] == 1

    l = 0
    c_f = 3 * D_ATT
    w_qkv = w_in[l][:, :c_f].astype(bf16)
    w_lru = w_in[l][:, c_f + N_HEADS:].astype(bf16)
    wft = jnp.pad(w_in[l][:, c_f:c_f + N_HEADS].T, ((0, F_ROWS - N_HEADS), (0, 0))).astype(bf16)
    wa = w_gate_a[l].astype(bf16)
    wx = w_gate_x[l].astype(bf16)
    wo = w_out[l].astype(bf16)
    wu = w_up[l]
    wd = w_down[l]
    lru_w = (w_conv[l], b_conv[l], wa, wx, b_gate_a[l], b_gate_x[l], lru_lambda[l], g_lru_out[l])

    xp = x_prompt.reshape(n_seq * seq, D_MODEL)
    q, k, v, kb, vb, xl, gl, lft = _in_proj(xp, g_mix[l], w_qkv, w_lru, wft, b_f[l], tm=256, q_dtype=bf16)
    lru_n, h_t = _lru(xl, gl, *lru_w, n_seq=n_seq, seq=seq, tb=256)
    att_n = _attn(q, kb, vb, lft, g_att_out[l], n_seq=n_seq, seq=seq, blk=256)
    x1, xn = _out_proj(att_n, lru_n, xp, wo, g_ffn[l], tm=512)
    y_prompt = _ffn(xn, x1, wu, wd, g_final, tm=1024, tf=512).reshape(n_seq, seq, D_MODEL)

    new_k_p = k.reshape(1, n_seq, seq, N_HEADS, HEAD_DIM)
    new_v_p = v.reshape(1, n_seq, seq, N_HEADS, HEAD_DIM)
    new_f_p = lft[:N_HEADS].T.reshape(1, n_seq, seq, N_HEADS)
    new_c_p = xl.reshape(n_seq, seq, D_LRU)[:, seq - (CONV_W - 1):, :][None]
    new_h_p = h_t.reshape(1, n_seq, D_LRU)

    xs = x_sample.reshape(n_dec, D_MODEL)
    qs, ks, vs, _, _, xls, gls, lfts = _in_proj(xs, g_mix[l], w_qkv, w_lru, wft, b_f[l], tm=n_dec,
                                                q_dtype=f32)
    lf_s = lfts[:N_HEADS].T
    heads = lambda t: t.reshape(n_dec, N_HEADS, HEAD_DIM)
    lf_wide = jnp.broadcast_to(lf_s[:, :, None], (n_dec, N_HEADS, HEAD_DIM))
    att_s = _decode(page_table, heads(qs), heads(ks), heads(vs), lf_wide, cache_k[l], cache_v[l],
                    jnp.swapaxes(cache_logf[l], 1, 2), g_att_out[l]).reshape(n_dec, D_ATT)
    sc = state_conv[l]
    lru_s, h_s = _lru_step(xls, gls, sc[:, 0], sc[:, 1], sc[:, 2], state_h[l], *lru_w)
    x1s, xns = _out_proj(att_s, lru_s, xs, wo, g_ffn[l], tm=n_dec)
    y_sample = _ffn(xns, x1s, wu, wd, g_final, tm=n_dec, tf=512).reshape(n_dec, 1, D_MODEL)

    new_k_s = ks.reshape(1, n_dec, 1, N_HEADS, HEAD_DIM)
    new_v_s = vs.reshape(1, n_dec, 1, N_HEADS, HEAD_DIM)
    new_f_s = lf_s.reshape(1, n_dec, 1, N_HEADS)
    new_c_s = jnp.stack([sc[:, 1], sc[:, 2], xls], axis=1)[None]
    new_h_s = h_s[None]

    return (y_prompt, y_sample, new_k_p, new_v_p, new_f_p, new_c_p, new_h_p,
            new_k_s, new_v_s, new_f_s, new_c_s, new_h_s)
```

```python
import functools

import jax
import jax.numpy as jnp
from jax import lax
from jax.experimental import pallas as pl
from jax.experimental.pallas import tpu as pltpu

D_MODEL = 2048
D_ATT = 1024
D_LRU = 1024
HEAD_DIM = 128
N_HEADS = 8
N_LRU_BLOCKS = 8
LRU_BLOCK = 128
CONV_W = 4
LRU_C = 8.0
D_FF = 4 * D_MODEL
EPS = 1e-6
NEG_INF = -1e30
F_ROWS = 16
SUBLANES = 8
LANES = 128
MIB = 1024 * 1024

f32 = jnp.float32
bf16 = jnp.bfloat16


def _rms(x, g):
    return x * lax.rsqrt(jnp.mean(x * x, axis=-1, keepdims=True) + EPS) * g


def _softplus(x):
    return jnp.maximum(x, 0.0) + jnp.log1p(jnp.exp(-jnp.abs(x)))


def _sigmoid(x):
    return 0.5 * jnp.tanh(0.5 * x) + 0.5


def _gelu_tanh(x):
    c = 0.7978845608028654
    return x * (0.5 * (1.0 + jnp.tanh(c * (x + 0.044715 * (x * x * x)))))


def _split3(x):
    hi = x.astype(bf16)
    r1 = x - hi.astype(f32)
    mid = r1.astype(bf16)
    lo = (r1 - mid.astype(f32)).astype(bf16)
    return hi, mid, lo


def _params(sem, vmem_mib, flags=None):
    return pltpu.CompilerParams(dimension_semantics=sem, vmem_limit_bytes=vmem_mib * MIB, flags=flags)


def _in_proj_kernel(x_ref, xs_ref, g_ref, wqkv_ref, wlru_ref, wft_ref, bf_ref,
                    q_ref, k_ref, v_ref, kb_ref, vb_ref, xl_ref, gl_ref, lft_ref,
                    qs_ref, ks_ref, vs_ref, xls_ref, gls_ref, lfts_ref):
    i = pl.program_id(0)
    last = pl.num_programs(0) - 1
    tm = x_ref.shape[0]

    def project(x):
        xn = _rms(x, g_ref[...]).astype(bf16)

        def cols(w_ref, c):
            return jnp.dot(xn, w_ref[:, c * D_ATT:(c + 1) * D_ATT], preferred_element_type=f32)

        t = lax.dot_general(wft_ref[...], xn, (((1,), (1,)), ((), ())),
                            preferred_element_type=f32) + bf_ref[:, :x.shape[0]]
        lf = jnp.minimum(t, 0.0) - jnp.log1p(jnp.exp(-jnp.abs(t)))
        return (cols(wqkv_ref, 0) * (HEAD_DIM ** -0.5), cols(wqkv_ref, 1), cols(wqkv_ref, 2),
                cols(wlru_ref, 0), cols(wlru_ref, 1), lf)

    def store_prompt(q, k, v, xl, gl, lf):
        q_ref[...] = q.astype(q_ref.dtype)
        k_ref[...] = k
        kb_ref[...] = k.astype(bf16)
        v_ref[...] = v
        vb_ref[...] = v.astype(bf16)
        xl_ref[...] = xl
        gl_ref[...] = gl
        lft_ref[...] = lf

    @pl.when(i < last)
    def _():
        store_prompt(*project(x_ref[...]))

    @pl.when(i == last)
    def _():
        q, k, v, xl, gl, lf = project(jnp.concatenate([x_ref[...], xs_ref[...]], axis=0))
        store_prompt(q[:tm], k[:tm], v[:tm], xl[:tm], gl[:tm], lf[:, :tm])
        qs_ref[...] = q[tm:]
        ks_ref[...] = k[tm:]
        vs_ref[...] = v[tm:]
        xls_ref[...] = xl[tm:]
        gls_ref[...] = gl[tm:]
        lfts_ref[...] = lf[:, tm:]


def _in_proj(x, x_s, g_mix, w_qkv, w_lru, wft, b_f, *, tm):
    m, ms = x.shape[0], x_s.shape[0]
    bfb = jnp.broadcast_to(jnp.pad(b_f, (0, F_ROWS - N_HEADS))[:, None], (F_ROWS, tm + ms)).astype(f32)
    row = lambda i: (i, 0)
    const = lambda i: (0, 0)
    wide = pl.BlockSpec((tm, D_ATT), row)
    small = pl.BlockSpec((ms, D_ATT), const)
    wide_shape = lambda dt: jax.ShapeDtypeStruct((m, D_ATT), dt)
    small_shape = jax.ShapeDtypeStruct((ms, D_ATT), f32)
    outs = pl.pallas_call(
        _in_proj_kernel,
        grid=(m // tm,),
        in_specs=[
            pl.BlockSpec((tm, D_MODEL), row),
            pl.BlockSpec((ms, D_MODEL), const),
            pl.BlockSpec((1, D_MODEL), const),
            pl.BlockSpec((D_MODEL, 3 * D_ATT), const, pipeline_mode=pl.Buffered(1)),
            pl.BlockSpec((D_MODEL, 2 * D_LRU), const, pipeline_mode=pl.Buffered(1)),
            pl.BlockSpec((F_ROWS, D_MODEL), const),
            pl.BlockSpec((F_ROWS, tm + ms), const),
        ],
        out_specs=(wide,) * 7 + (pl.BlockSpec((F_ROWS, tm), lambda i: (0, i)),)
        + (small,) * 5 + (pl.BlockSpec((F_ROWS, ms), const),),
        out_shape=(wide_shape(bf16), wide_shape(f32), wide_shape(f32), wide_shape(bf16), wide_shape(bf16),
                   wide_shape(f32), wide_shape(f32), jax.ShapeDtypeStruct((F_ROWS, m), f32))
        + (small_shape,) * 5 + (jax.ShapeDtypeStruct((F_ROWS, ms), f32),),
        compiler_params=_params(("arbitrary",), 52),
        name="in_proj",
    )(x, x_s, g_mix.reshape(1, D_MODEL), w_qkv, w_lru, wft, bfb)
    return outs[:8], outs[8:]


def _lru_gates(xc, wa_ref, wx_ref, ba, bx, lam):
    xcb = xc.astype(bf16)
    r_parts, i_parts = [], []
    for n in range(N_LRU_BLOCKS):
        blk = xcb[:, n * LRU_BLOCK:(n + 1) * LRU_BLOCK]
        r_parts.append(jnp.dot(blk, wa_ref[n], preferred_element_type=f32))
        i_parts.append(jnp.dot(blk, wx_ref[n], preferred_element_type=f32))
    r = _sigmoid(jnp.concatenate(r_parts, axis=1) + ba)
    i = _sigmoid(jnp.concatenate(i_parts, axis=1) + bx)
    log_a = (-LRU_C * r) * _softplus(-lam)
    a = jnp.exp(log_a)
    th = jnp.tanh(log_a)
    b = jnp.sqrt(-2.0 * th) * lax.rsqrt(1.0 - th) * i * xc
    return a, b


def _lru_kernel(xl_ref, gl_ref, wc_ref, bc_ref, wa_ref, wx_ref, ba_ref, bx_ref, lam_ref, g_ref,
                out_ref, ht_ref, ext_ref, h_ref, hs_ref):
    c = pl.program_id(1)
    tb = xl_ref.shape[0]

    @pl.when(c == 0)
    def _():
        ext_ref[0:SUBLANES, :] = jnp.zeros((SUBLANES, D_LRU), f32)
        h_ref[...] = jnp.zeros((SUBLANES, D_LRU), f32)

    @pl.when(c > 0)
    def _():
        ext_ref[0:SUBLANES, :] = ext_ref[tb:tb + SUBLANES, :]

    x = xl_ref[...]
    ext_ref[SUBLANES:tb + SUBLANES, :] = x
    wc = wc_ref[...]
    xc = bc_ref[...] + wc[0:1] * ext_ref[5:5 + tb, :]
    xc = xc + wc[1:2] * ext_ref[6:6 + tb, :]
    xc = xc + wc[2:3] * ext_ref[7:7 + tb, :]
    xc = xc + wc[3:4] * x

    a, b = _lru_gates(xc, wa_ref, wx_ref, ba_ref[...], bx_ref[...], lam_ref[...])

    row = lax.broadcasted_iota(jnp.int32, (tb, D_LRU), 0) & (SUBLANES - 1)
    for s in (1, 2, 4):
        a_sh = pltpu.roll(a, s, 0)
        b_sh = pltpu.roll(b, s, 0)
        take = row >= s
        b = jnp.where(take, a * b_sh + b, b)
        a = jnp.where(take, a * a_sh, a)

    carry = h_ref[...]
    for g in range(tb // SUBLANES):
        sl = slice(g * SUBLANES, (g + 1) * SUBLANES)
        hg = a[sl] * carry + b[sl]
        hs_ref[sl, :] = hg
        carry = jnp.broadcast_to(hg[SUBLANES - 1:SUBLANES, :], (SUBLANES, D_LRU))
    h_ref[...] = carry

    lru = _gelu_tanh(gl_ref[...]) * hs_ref[...]
    out_ref[...] = _rms(lru, g_ref[...]).astype(out_ref.dtype)

    @pl.when(c == pl.num_programs(1) - 1)
    def _():
        ht_ref[0] = carry[0:1]


def _lru(xl, gl, w_conv, b_conv, wa, wx, ba, bx, lam, g_lru, *, n_seq, seq, tb):
    nc = seq // tb
    row = lambda b, c: (b * nc + c, 0)
    const2 = lambda b, c: (0, 0)
    const3 = lambda b, c: (0, 0, 0)
    vec = pl.BlockSpec((1, D_LRU), const2)
    return pl.pallas_call(
        _lru_kernel,
        grid=(n_seq, nc),
        in_specs=[
            pl.BlockSpec((tb, D_LRU), row), pl.BlockSpec((tb, D_LRU), row),
            pl.BlockSpec((CONV_W, D_LRU), const2), vec,
            pl.BlockSpec((N_LRU_BLOCKS, LRU_BLOCK, LRU_BLOCK), const3),
            pl.BlockSpec((N_LRU_BLOCKS, LRU_BLOCK, LRU_BLOCK), const3),
            vec, vec, vec, vec,
        ],
        out_specs=(pl.BlockSpec((tb, D_LRU), row),
                   pl.BlockSpec((1, 1, D_LRU), lambda b, c: (b, 0, 0))),
        out_shape=(jax.ShapeDtypeStruct((n_seq * seq, D_LRU), bf16),
                   jax.ShapeDtypeStruct((n_seq, 1, D_LRU), f32)),
        scratch_shapes=[pltpu.VMEM((tb + SUBLANES, D_LRU), f32),
                        pltpu.VMEM((SUBLANES, D_LRU), f32),
                        pltpu.VMEM((tb, D_LRU), f32)],
        compiler_params=_params(("parallel", "arbitrary"), 40),
        name="lru",
    )(xl, gl, w_conv, b_conv.reshape(1, D_LRU), wa, wx, ba.reshape(1, D_LRU),
      bx.reshape(1, D_LRU), lam.reshape(1, D_LRU), g_lru.reshape(1, D_LRU))


def _lru_step_kernel(xl_ref, gl_ref, s0_ref, s1_ref, s2_ref, h0_ref, wc_ref, bc_ref, wa_ref, wx_ref,
                     ba_ref, bx_ref, lam_ref, g_ref, out_ref, h_ref):
    wc = wc_ref[...]
    xc = bc_ref[...] + wc[0:1] * s0_ref[...]
    xc = xc + wc[1:2] * s1_ref[...]
    xc = xc + wc[2:3] * s2_ref[...]
    xc = xc + wc[3:4] * xl_ref[...]
    a, b = _lru_gates(xc, wa_ref, wx_ref, ba_ref[...], bx_ref[...], lam_ref[...])
    h = a * h0_ref[...] + b
    h_ref[...] = h
    out_ref[...] = _rms(_gelu_tanh(gl_ref[...]) * h, g_ref[...]).astype(out_ref.dtype)


def _lru_step(xl, gl, s0, s1, s2, h0, w_conv, b_conv, wa, wx, ba, bx, lam, g_lru):
    n = xl.shape[0]
    return pl.pallas_call(
        _lru_step_kernel,
        out_shape=(jax.ShapeDtypeStruct((n, D_LRU), bf16), jax.ShapeDtypeStruct((n, D_LRU), f32)),
        name="lru_step",
    )(xl, gl, s0, s1, s2, h0, w_conv, b_conv.reshape(1, D_LRU), wa, wx, ba.reshape(1, D_LRU),
      bx.reshape(1, D_LRU), lam.reshape(1, D_LRU), g_lru.reshape(1, D_LRU))


def _attn_kernel(q_ref, k_ref, v_ref, lft_ref, g_ref, o_ref, nf_ref, att_ref, m_ref, l_ref, *, blk, kblk):
    qi = pl.program_id(1)
    n_kblk = nf_ref.shape[0]
    per_q = blk // kblk

    @pl.when(qi == 0)
    def _():
        r = lax.broadcasted_iota(jnp.int32, (kblk, 2 * kblk), 0)
        c = lax.broadcasted_iota(jnp.int32, (kblk, 2 * kblk), 1)
        tri_ones = jnp.where((r <= c) | (c >= kblk), 1.0, 0.0).astype(bf16)
        carry = jnp.zeros((F_ROWS, kblk), f32)
        for j in range(n_kblk):
            hi, mid, lo = _split3(lft_ref[:, j * kblk:(j + 1) * kblk])
            res = (jnp.dot(hi, tri_ones, preferred_element_type=f32)
                   + jnp.dot(mid, tri_ones, preferred_element_type=f32)
                   + jnp.dot(lo, tri_ones, preferred_element_type=f32))
            cum = carry + res[:, :kblk]
            nf_ref[j] = -cum[0:N_HEADS]
            carry = carry + res[:, kblk:]

    rows = lax.broadcasted_iota(jnp.int32, (blk, kblk), 0)
    colsi = lax.broadcasted_iota(jnp.int32, (blk, kblk), 1)

    m_ref[...] = jnp.full(m_ref.shape, NEG_INF, f32)
    l_ref[...] = jnp.zeros(l_ref.shape, f32)
    att_ref[...] = jnp.zeros(att_ref.shape, f32)

    def kv_block(j, diag):
        off = pl.multiple_of(j * kblk, kblk)
        nf = nf_ref[j]
        for h in range(N_HEADS):
            hs = slice(h * HEAD_DIM, (h + 1) * HEAD_DIM)
            s = lax.dot_general(q_ref[:, hs], k_ref[pl.ds(off, kblk), hs], (((1,), (1,)), ((), ())),
                                preferred_element_type=f32) + nf[h:h + 1, :]
            if diag is not None:
                s = jnp.where(colsi + diag * kblk <= rows, s, NEG_INF)
            m_old = m_ref[h]
            m_new = jnp.maximum(m_old, jnp.max(s, axis=-1, keepdims=True))
            alpha = jnp.exp(m_old - m_new)
            p = jnp.exp(s - jnp.concatenate([m_new] * (kblk // LANES), axis=1))
            m_ref[h] = m_new
            l_ref[h] = alpha * l_ref[h] + jnp.sum(p, axis=-1, keepdims=True)
            att_ref[:, hs] = alpha * att_ref[:, hs] + jnp.dot(
                p.astype(bf16), v_ref[pl.ds(off, kblk), hs], preferred_element_type=f32)

    def body(j, carry):
        kv_block(j, None)
        return carry

    lax.fori_loop(0, qi * per_q, body, 0)
    for d in range(per_q):
        kv_block(qi * per_q + d, d)
    for h in range(N_HEADS):
        hs = slice(h * HEAD_DIM, (h + 1) * HEAD_DIM)
        att_ref[:, hs] = att_ref[:, hs] / l_ref[h]
    o_ref[...] = _rms(att_ref[...], g_ref[...]).astype(o_ref.dtype)


def _attn(q, kb, vb, lft, g_att, *, n_seq, seq, blk, kblk):
    nq = seq // blk
    return pl.pallas_call(
        functools.partial(_attn_kernel, blk=blk, kblk=kblk),
        grid=(n_seq, nq),
        in_specs=[
            pl.BlockSpec((blk, D_ATT), lambda b, i: (b * nq + i, 0)),
            pl.BlockSpec((seq, D_ATT), lambda b, i: (b, 0)),
            pl.BlockSpec((seq, D_ATT), lambda b, i: (b, 0)),
            pl.BlockSpec((F_ROWS, seq), lambda b, i: (0, b)),
            pl.BlockSpec((1, D_ATT), lambda b, i: (0, 0)),
        ],
        out_specs=pl.BlockSpec((blk, D_ATT), lambda b, i: (b * nq + i, 0)),
        out_shape=jax.ShapeDtypeStruct((n_seq * seq, D_ATT), bf16),
        scratch_shapes=[pltpu.VMEM((seq // kblk, N_HEADS, kblk), f32), pltpu.VMEM((blk, D_ATT), f32),
                        pltpu.VMEM((N_HEADS, blk, LANES), f32), pltpu.VMEM((N_HEADS, blk, LANES), f32)],
        compiler_params=_params(("parallel", "arbitrary"), 40),
        name="attn",
    )(q, kb, vb, lft, g_att.reshape(1, D_ATT))


PAGES_PER_STEP = 2
RING_STEPS = 8
FETCH_AHEAD = RING_STEPS - 2
PAGE_SLOTS = RING_STEPS * PAGES_PER_STEP
N_PARTIAL = 4
TIE_LAG = 24


def _tree_sum(xs):
    while len(xs) > 1:
        xs = [a + b for a, b in zip(xs[0::2], xs[1::2])] + xs[len(xs) & ~1:]
    return xs[0]


def _lockstep(lead, follow):
    next(follow)
    lead_result = follow_result = None
    while True:
        try:
            fresh = next(lead)
        except StopIteration as done:
            lead_result = done.value
            break
        try:
            follow.send(fresh)
        except StopIteration as done:
            follow_result = done.value
    return lead_result, follow_result


def _alternate(*passes):
    results = [None] * len(passes)
    live = list(range(len(passes)))
    while live:
        for i in list(live):
            try:
                next(passes[i])
            except StopIteration as done:
                results[i] = done.value
                live.remove(i)
    return results


def _decode_kernel(pt_ref, q_ref, kn_ref, vn_ref, lfn_ref, g_ref, k_hbm, v_hbm, lf_hbm,
                   o_ref, kbuf, vbuf, lfbuf, sc_a, sc_b, tri_ref, sem, *, n_pages):
    page = kbuf.shape[1]
    steps_per_seq = n_pages // PAGES_PER_STEP
    n_steps = o_ref.shape[0] * steps_per_seq
    last_of_seq = steps_per_seq - 1

    def page_copies(slot, pid):
        return (pltpu.make_async_copy(k_hbm.at[pid], kbuf.at[slot], sem.at[0, slot]),
                pltpu.make_async_copy(v_hbm.at[pid], vbuf.at[slot], sem.at[1, slot]),
                pltpu.make_async_copy(lf_hbm.at[pid], lfbuf.at[slot], sem.at[2, slot]))

    def slot_base(step):
        return (step % RING_STEPS) * PAGES_PER_STEP

    def start_fetch(step):
        for u in range(PAGES_PER_STEP):
            for cp in page_copies(slot_base(step) + u, pt_ref[step * PAGES_PER_STEP + u]):
                cp.start()

    def wait_fetch(step):
        for u in range(PAGES_PER_STEP):
            for cp in page_copies(slot_base(step) + u, 0):
                cp.wait()

    r = lax.broadcasted_iota(jnp.int32, (page, 2 * page), 0)
    c = lax.broadcasted_iota(jnp.int32, (page, 2 * page), 1)
    tri_ref[...] = jnp.where((r <= c) | (c >= page), 1.0, 0.0).astype(bf16)
    lane = lax.broadcasted_iota(jnp.int32, (N_HEADS, LANES), 1)
    never = pt_ref[0] < 0

    def score_pass(step, ring, car, sc_ref):
        q = q_ref[step // steps_per_seq]
        base = ring * PAGES_PER_STEP
        car = jnp.where(step % steps_per_seq == 0, 0.0, car)
        lf2 = jnp.concatenate([lfbuf[base + u] for u in range(PAGES_PER_STEP)], axis=0)
        hi, mid, lo = _split3(lf2)
        tri_ones = tri_ref[...]
        res = (jnp.dot(hi, tri_ones, preferred_element_type=f32)
               + jnp.dot(mid, tri_ones, preferred_element_type=f32)
               + jnp.dot(lo, tri_ones, preferred_element_type=f32))
        cars, m_loc = [], []
        for u in range(PAGES_PER_STEP):
            rows = slice(u * N_HEADS, (u + 1) * N_HEADS)
            cum = res[rows, :page]
            part = [None] * N_PARTIAL
            for s in range(page):
                t = kbuf[base + u, s] * q - jnp.where(lane == s, cum, 0.0)
                sc = jnp.broadcast_to(jnp.sum(t, axis=-1, keepdims=True), (N_HEADS, LANES))
                sc_ref[u, s] = sc
                i = s % N_PARTIAL
                part[i] = sc if part[i] is None else jnp.maximum(part[i], sc)
                yield sc
            cars.append(car)
            m_loc.append(functools.reduce(jnp.maximum, part) - car)
            car = car + res[rows, page:]
        return tuple(m_loc), tuple(cars), car

    def value_pass(step, ring, soft, scored, sc_ref, tie_ref):
        m_loc, cars, _ = scored
        base = ring * PAGES_PER_STEP
        first = step % steps_per_seq == 0
        m_run = jnp.where(first, NEG_INF, soft[0])
        l_run = jnp.where(first, 0.0, soft[1])
        a_run = jnp.where(first, 0.0, soft[2])
        m_new = functools.reduce(jnp.maximum, list(m_loc) + [m_run])
        alpha = jnp.exp(m_run - m_new)
        l_parts, a_parts = [l_run * alpha], [a_run * alpha]
        recent = []
        for u in range(PAGES_PER_STEP):
            shift = m_new + cars[u]
            lp = [None] * N_PARTIAL
            ap = [None] * N_PARTIAL
            for s in range(page):
                recent.append((yield))
                logit = sc_ref[u, s]
                if len(recent) > TIE_LAG:
                    logit = jnp.where(never, recent.pop(0), logit)
                pr = jnp.exp(logit - shift)
                pv = pr * vbuf[base + u, s]
                i = s % N_PARTIAL
                lp[i] = pr if lp[i] is None else lp[i] + pr
                ap[i] = pv if ap[i] is None else ap[i] + pv
            l_parts += lp
            a_parts += ap
        return m_new, _tree_sum(l_parts), _tree_sum(a_parts)

    def write_output(step, soft, car_end):
        b = step // steps_per_seq
        m_run, l_run, a_run = soft

        @pl.when(step % steps_per_seq == last_of_seq)
        def _():
            q = q_ref[b]
            s_new = jnp.sum(kn_ref[b] * q, axis=-1, keepdims=True) - (car_end + lfn_ref[b])
            m_all = jnp.maximum(m_run, s_new)
            w = jnp.exp(m_run - m_all)
            w_new = jnp.exp(s_new - m_all)
            o = (a_run * w + w_new * vn_ref[b]) / (l_run * w + w_new)
            ms = jnp.sum(jnp.sum(o * o, axis=-1, keepdims=True), axis=0, keepdims=True) / D_ATT
            o_ref[b] = o * lax.rsqrt(ms + EPS) * g_ref[...]

    def half(step, ring, soft, scored):
        nxt = step + 1
        sc_cur, sc_nxt = (sc_a, sc_b) if ring % 2 == 0 else (sc_b, sc_a)

        @pl.when(nxt + FETCH_AHEAD < n_steps)
        def _():
            start_fetch(nxt + FETCH_AHEAD)

        @pl.when(nxt < n_steps)
        def _():
            wait_fetch(nxt)

        scored_nxt, soft = _lockstep(
            score_pass(jnp.minimum(nxt, n_steps - 1), (ring + 1) % RING_STEPS, scored[2], sc_nxt),
            value_pass(step, ring, soft, scored, sc_cur, sc_nxt))
        write_output(step, soft, scored[2])
        return soft, scored_nxt

    for step in range(min(FETCH_AHEAD + 1, n_steps)):
        start_fetch(step)
    wait_fetch(0)
    zero = jnp.zeros((N_HEADS, LANES), f32)
    scored0, = _alternate(score_pass(0, 0, zero, sc_a))

    def trip(t, carry):
        soft, scored = carry
        for ring in range(RING_STEPS):
            soft, scored = half(t * RING_STEPS + ring, ring, soft, scored)
        return soft, scored

    lax.fori_loop(0, n_steps // RING_STEPS, trip, ((zero, zero, zero), scored0))


def _decode(page_table, q, k_new, v_new, lf_new, cache_k, cache_v, cache_lf_t, g_att):
    n_seq, n_pages = page_table.shape
    page = cache_k.shape[1]
    assert n_pages % PAGES_PER_STEP == 0 and page == LANES
    assert (n_seq * n_pages // PAGES_PER_STEP) % RING_STEPS == 0
    pt = page_table.reshape(-1)
    tok = pl.BlockSpec((n_seq, N_HEADS, HEAD_DIM), lambda i, pt: (0, 0, 0))
    hbm = pl.BlockSpec(memory_space=pl.ANY)
    grid_spec = pltpu.PrefetchScalarGridSpec(
        num_scalar_prefetch=1,
        grid=(1,),
        in_specs=[tok, tok, tok, tok, pl.BlockSpec((N_HEADS, HEAD_DIM), lambda i, pt: (0, 0)),
                  hbm, hbm, hbm],
        out_specs=tok,
        scratch_shapes=[pltpu.VMEM((PAGE_SLOTS, page, N_HEADS, HEAD_DIM), f32),
                        pltpu.VMEM((PAGE_SLOTS, page, N_HEADS, HEAD_DIM), f32),
                        pltpu.VMEM((PAGE_SLOTS, N_HEADS, page), f32),
                        pltpu.VMEM((PAGES_PER_STEP, page, N_HEADS, LANES), f32),
                        pltpu.VMEM((PAGES_PER_STEP, page, N_HEADS, LANES), f32),
                        pltpu.VMEM((page, 2 * page), bf16),
                        pltpu.SemaphoreType.DMA((3, PAGE_SLOTS))],
    )
    return pl.pallas_call(
        functools.partial(_decode_kernel, n_pages=n_pages),
        grid_spec=grid_spec,
        out_shape=jax.ShapeDtypeStruct((n_seq, N_HEADS, HEAD_DIM), f32),
        compiler_params=_params(("arbitrary",), 32),
        name="decode",
    )(pt, q, k_new, v_new, lf_new, g_att.reshape(N_HEADS, HEAD_DIM), cache_k, cache_v, cache_lf_t)


def _out_proj_kernel(att_ref, lru_ref, x_ref, w_ref, g_ref, x1_ref, xn_ref):
    mix = jnp.concatenate([att_ref[...].astype(bf16), lru_ref[...].astype(bf16)], axis=1)
    x1 = x_ref[...] + jnp.dot(mix, w_ref[...], preferred_element_type=f32)
    x1_ref[...] = x1
    xn_ref[...] = _rms(x1, g_ref[...]).astype(bf16)


def _out_proj(att_n, lru_n, x, w_out, g_ffn, *, tm):
    m = x.shape[0]
    row = lambda i: (i, 0)
    const = lambda i: (0, 0)
    return pl.pallas_call(
        _out_proj_kernel,
        grid=(m // tm,),
        in_specs=[pl.BlockSpec((tm, D_ATT), row), pl.BlockSpec((tm, D_LRU), row),
                  pl.BlockSpec((tm, D_MODEL), row),
                  pl.BlockSpec((D_MODEL, D_MODEL), const, pipeline_mode=pl.Buffered(1)),
                  pl.BlockSpec((1, D_MODEL), const)],
        out_specs=(pl.BlockSpec((tm, D_MODEL), row), pl.BlockSpec((tm, D_MODEL), row)),
        out_shape=(jax.ShapeDtypeStruct((m, D_MODEL), f32), jax.ShapeDtypeStruct((m, D_MODEL), bf16)),
        compiler_params=_params(("parallel",), 48),
        name="out_proj",
    )(att_n, lru_n, x, w_out, g_ffn.reshape(1, D_MODEL))


def _ffn_kernel(xn_ref, x1_ref, xns_ref, x1s_ref, wu_ref, wd_ref, g_ref, o_ref, os_ref):
    i = pl.program_id(0)
    f = pl.program_id(1)
    last_i = pl.num_programs(0) - 1
    last_f = pl.num_programs(1) - 1
    tm = xn_ref.shape[0]

    def mlp(x):
        hf = jnp.dot(x, wu_ref[...].astype(bf16), preferred_element_type=f32)
        return jnp.dot(jnp.square(jnp.maximum(hf, 0.0)).astype(bf16), wd_ref[...].astype(bf16),
                       preferred_element_type=f32)

    @pl.when(f == 0)
    def _():
        o_ref[...] = x1_ref[...]

    @pl.when(i < last_i)
    def _():
        o_ref[...] += mlp(xn_ref[...])

    @pl.when(i == last_i)
    def _():
        @pl.when(f == 0)
        def _():
            os_ref[...] = x1s_ref[...]

        y = mlp(jnp.concatenate([xn_ref[...], xns_ref[...]], axis=0))
        o_ref[...] += y[:tm]
        os_ref[...] += y[tm:]

        @pl.when(f == last_f)
        def _():
            os_ref[...] = _rms(os_ref[...], g_ref[...])

    @pl.when(f == last_f)
    def _():
        o_ref[...] = _rms(o_ref[...], g_ref[...])


def _ffn(xn, x1, xn_s, x1_s, w_up, w_down, g_final, *, tm, tf):
    m, ms = x1.shape[0], x1_s.shape[0]
    row = lambda i, f: (i, 0)
    const = lambda i, f: (0, 0)
    return pl.pallas_call(
        _ffn_kernel,
        grid=(m // tm, D_FF // tf),
        in_specs=[pl.BlockSpec((tm, D_MODEL), row, pipeline_mode=pl.Buffered(1)),
                  pl.BlockSpec((tm, D_MODEL), row, pipeline_mode=pl.Buffered(1)),
                  pl.BlockSpec((ms, D_MODEL), const), pl.BlockSpec((ms, D_MODEL), const),
                  pl.BlockSpec((D_MODEL, tf), lambda i, f: (0, f)),
                  pl.BlockSpec((tf, D_MODEL), lambda i, f: (f, 0)),
                  pl.BlockSpec((1, D_MODEL), const)],
        out_specs=(pl.BlockSpec((tm, D_MODEL), row), pl.BlockSpec((ms, D_MODEL), const)),
        out_shape=(jax.ShapeDtypeStruct((m, D_MODEL), f32), jax.ShapeDtypeStruct((ms, D_MODEL), f32)),
        compiler_params=_params(("arbitrary", "arbitrary"), 56),
        name="ffn",
    )(xn, x1, xn_s, x1_s, w_up, w_down, g_final.reshape(1, D_MODEL))


def kernel(x_prompt, x_sample, cache_k, cache_v, cache_logf, state_conv, state_h, page_table,
           g_mix, w_in, b_f, w_conv, b_conv, w_gate_a, b_gate_a, w_gate_x, b_gate_x,
           lru_lambda, g_att_out, g_lru_out, w_out, g_ffn, w_up, w_down, g_final):
    depth = w_in.shape[0]
    n_seq, seq, _ = x_prompt.shape
    n_dec = x_sample.shape[0]
    assert depth == 1 and x_sample.shape[1] == 1

    l = 0
    c_f = 3 * D_ATT
    w_qkv = w_in[l][:, :c_f].astype(bf16)
    w_lru = w_in[l][:, c_f + N_HEADS:].astype(bf16)
    wft = jnp.pad(w_in[l][:, c_f:c_f + N_HEADS].T, ((0, F_ROWS - N_HEADS), (0, 0))).astype(bf16)
    wa = w_gate_a[l].astype(bf16)
    wx = w_gate_x[l].astype(bf16)
    wo = w_out[l].astype(bf16)
    wu = w_up[l]
    wd = w_down[l]
    lru_w = (w_conv[l], b_conv[l], wa, wx, b_gate_a[l], b_gate_x[l], lru_lambda[l], g_lru_out[l])

    xp = x_prompt.reshape(n_seq * seq, D_MODEL)
    xs = x_sample.reshape(n_dec, D_MODEL)
    (q, k, v, kb, vb, xl, gl, lft), (qs, ks, vs, xls, gls, lfts) = _in_proj(
        xp, xs, g_mix[l], w_qkv, w_lru, wft, b_f[l], tm=256)
    lru_n, h_t = _lru(xl, gl, *lru_w, n_seq=n_seq, seq=seq, tb=256)
    att_n = _attn(q, kb, vb, lft, g_att_out[l], n_seq=n_seq, seq=seq, blk=256, kblk=256)
    x1, xn = _out_proj(att_n, lru_n, xp, wo, g_ffn[l], tm=512)

    new_k_p = k.reshape(1, n_seq, seq, N_HEADS, HEAD_DIM)
    new_v_p = v.reshape(1, n_seq, seq, N_HEADS, HEAD_DIM)
    new_f_p = lft[:N_HEADS].T.reshape(1, n_seq, seq, N_HEADS)
    new_c_p = xl.reshape(n_seq, seq, D_LRU)[:, seq - (CONV_W - 1):, :][None]
    new_h_p = h_t.reshape(1, n_seq, D_LRU)

    lf_s = lfts[:N_HEADS].T
    heads = lambda t: t.reshape(n_dec, N_HEADS, HEAD_DIM)
    lf_wide = jnp.broadcast_to(lf_s[:, :, None], (n_dec, N_HEADS, HEAD_DIM))
    att_s = _decode(page_table, heads(qs), heads(ks), heads(vs), lf_wide, cache_k[l], cache_v[l],
                    jnp.swapaxes(cache_logf[l], 1, 2), g_att_out[l]).reshape(n_dec, D_ATT)
    sc = state_conv[l]
    lru_s, h_s = _lru_step(xls, gls, sc[:, 0], sc[:, 1], sc[:, 2], state_h[l], *lru_w)
    x1s, xns = _out_proj(att_s, lru_s, xs, wo, g_ffn[l], tm=n_dec)

    y_prompt, y_sample = _ffn(xn, x1, xns, x1s, wu, wd, g_final, tm=1024, tf=512)
    y_prompt = y_prompt.reshape(n_seq, seq, D_MODEL)
    y_sample = y_sample.reshape(n_dec, 1, D_MODEL)

    new_k_s = ks.reshape(1, n_dec, 1, N_HEADS, HEAD_DIM)
    new_v_s = vs.reshape(1, n_dec, 1, N_HEADS, HEAD_DIM)
    new_f_s = lf_s.reshape(1, n_dec, 1, N_HEADS)
    new_c_s = jnp.stack([sc[:, 1], sc[:, 2], xls], axis=1)[None]
    new_h_s = h_s[None]

    return (y_prompt, y_sample, new_k_p, new_v_p, new_f_p, new_c_p, new_h_p,
            new_k_s, new_v_s, new_f_s, new_c_s, new_h_s)
```

```python
import functools

import jax
import jax.numpy as jnp
from jax import lax
from jax.experimental import pallas as pl
from jax.experimental.pallas import tpu as pltpu

D_MODEL = 2048
D_ATT = 1024
D_LRU = 1024
HEAD_DIM = 128
N_HEADS = 8
N_LRU_BLOCKS = 8
LRU_BLOCK = 128
CONV_W = 4
LRU_C = 8.0
D_FF = 4 * D_MODEL
EPS = 1e-6
NEG_INF = -1e30
LOG2E = 1.4426950408889634
F_ROWS = 16
SUBLANES = 8
LANES = 128
MIB = 1024 * 1024

f32 = jnp.float32
bf16 = jnp.bfloat16


def _rms(x, g):
    return x * lax.rsqrt(jnp.mean(x * x, axis=-1, keepdims=True) + EPS) * g


def _softplus(x):
    return jnp.maximum(x, 0.0) + jnp.log1p(jnp.exp(-jnp.abs(x)))


def _sigmoid(x):
    return 0.5 * jnp.tanh(0.5 * x) + 0.5


def _gelu_tanh(x):
    c = 0.7978845608028654
    return x * (0.5 * (1.0 + jnp.tanh(c * (x + 0.044715 * (x * x * x)))))


def _split3(x):
    hi = x.astype(bf16)
    r1 = x - hi.astype(f32)
    mid = r1.astype(bf16)
    lo = (r1 - mid.astype(f32)).astype(bf16)
    return hi, mid, lo


def _params(sem, vmem_mib, flags=None):
    return pltpu.CompilerParams(dimension_semantics=sem, vmem_limit_bytes=vmem_mib * MIB, flags=flags)


def _in_proj_kernel(x_ref, xs_ref, g_ref, wqkv_ref, wlru_ref, wft_ref, bf_ref,
                    q_ref, k_ref, v_ref, kb_ref, vb_ref, xl_ref, gl_ref, lft_ref,
                    qs_ref, ks_ref, vs_ref, xls_ref, gls_ref, lfts_ref):
    i = pl.program_id(0)
    last = pl.num_programs(0) - 1
    tm = x_ref.shape[0]

    def project(x):
        xn = _rms(x, g_ref[...]).astype(bf16)

        def cols(w_ref, c):
            return jnp.dot(xn, w_ref[:, c * D_ATT:(c + 1) * D_ATT], preferred_element_type=f32)

        t = lax.dot_general(wft_ref[...], xn, (((1,), (1,)), ((), ())),
                            preferred_element_type=f32) + bf_ref[:, :x.shape[0]]
        lf = jnp.minimum(t, 0.0) - jnp.log1p(jnp.exp(-jnp.abs(t)))
        return (cols(wqkv_ref, 0) * (HEAD_DIM ** -0.5), cols(wqkv_ref, 1), cols(wqkv_ref, 2),
                cols(wlru_ref, 0), cols(wlru_ref, 1), lf)

    def store_prompt(q, k, v, xl, gl, lf):
        q_ref[...] = q.astype(q_ref.dtype)
        k_ref[...] = k
        kb_ref[...] = k.astype(bf16)
        v_ref[...] = v
        vb_ref[...] = v.astype(bf16)
        xl_ref[...] = xl
        gl_ref[...] = gl
        lft_ref[...] = lf

    @pl.when(i < last)
    def _():
        store_prompt(*project(x_ref[...]))

    @pl.when(i == last)
    def _():
        q, k, v, xl, gl, lf = project(jnp.concatenate([x_ref[...], xs_ref[...]], axis=0))
        store_prompt(q[:tm], k[:tm], v[:tm], xl[:tm], gl[:tm], lf[:, :tm])
        qs_ref[...] = q[tm:]
        ks_ref[...] = k[tm:]
        vs_ref[...] = v[tm:]
        xls_ref[...] = xl[tm:]
        gls_ref[...] = gl[tm:]
        lfts_ref[...] = lf[:, tm:]


def _in_proj(x, x_s, g_mix, w_qkv, w_lru, wft, b_f, *, tm):
    m, ms = x.shape[0], x_s.shape[0]
    bfb = jnp.broadcast_to(jnp.pad(b_f, (0, F_ROWS - N_HEADS))[:, None], (F_ROWS, tm + ms)).astype(f32)
    row = lambda i: (i, 0)
    const = lambda i: (0, 0)
    wide = pl.BlockSpec((tm, D_ATT), row)
    small = pl.BlockSpec((ms, D_ATT), const)
    wide_shape = lambda dt: jax.ShapeDtypeStruct((m, D_ATT), dt)
    small_shape = jax.ShapeDtypeStruct((ms, D_ATT), f32)
    outs = pl.pallas_call(
        _in_proj_kernel,
        grid=(m // tm,),
        in_specs=[
            pl.BlockSpec((tm, D_MODEL), row),
            pl.BlockSpec((ms, D_MODEL), const),
            pl.BlockSpec((1, D_MODEL), const),
            pl.BlockSpec((D_MODEL, 3 * D_ATT), const, pipeline_mode=pl.Buffered(1)),
            pl.BlockSpec((D_MODEL, 2 * D_LRU), const, pipeline_mode=pl.Buffered(1)),
            pl.BlockSpec((F_ROWS, D_MODEL), const),
            pl.BlockSpec((F_ROWS, tm + ms), const),
        ],
        out_specs=(wide,) * 7 + (pl.BlockSpec((F_ROWS, tm), lambda i: (0, i)),)
        + (small,) * 5 + (pl.BlockSpec((F_ROWS, ms), const),),
        out_shape=(wide_shape(bf16), wide_shape(f32), wide_shape(f32), wide_shape(bf16), wide_shape(bf16),
                   wide_shape(f32), wide_shape(f32), jax.ShapeDtypeStruct((F_ROWS, m), f32))
        + (small_shape,) * 5 + (jax.ShapeDtypeStruct((F_ROWS, ms), f32),),
        compiler_params=_params(("arbitrary",), 52),
        name="in_proj",
    )(x, x_s, g_mix.reshape(1, D_MODEL), w_qkv, w_lru, wft, bfb)
    return outs[:8], outs[8:]


def _lru_gates(xc, wa_ref, wx_ref, ba, bx, lam):
    xcb = xc.astype(bf16)
    r_parts, i_parts = [], []
    for n in range(N_LRU_BLOCKS):
        blk = xcb[:, n * LRU_BLOCK:(n + 1) * LRU_BLOCK]
        r_parts.append(jnp.dot(blk, wa_ref[n], preferred_element_type=f32))
        i_parts.append(jnp.dot(blk, wx_ref[n], preferred_element_type=f32))
    r = _sigmoid(jnp.concatenate(r_parts, axis=1) + ba)
    i = _sigmoid(jnp.concatenate(i_parts, axis=1) + bx)
    log_a = (-LRU_C * r) * _softplus(-lam)
    a = jnp.exp(log_a)
    th = jnp.tanh(log_a)
    u = -2.0 * th
    b = jnp.where(u > 0.0, u * lax.rsqrt(u * (1.0 - th)), 0.0) * i * xc
    return a, b


def _lru_kernel(xl_ref, gl_ref, wc_ref, bc_ref, wa_ref, wx_ref, ba_ref, bx_ref, lam_ref, g_ref,
                out_ref, ht_ref, ext_ref, h_ref, hs_ref):
    c = pl.program_id(1)
    tb = xl_ref.shape[0]

    @pl.when(c == 0)
    def _():
        ext_ref[0:SUBLANES, :] = jnp.zeros((SUBLANES, D_LRU), f32)
        h_ref[...] = jnp.zeros((SUBLANES, D_LRU), f32)

    @pl.when(c > 0)
    def _():
        ext_ref[0:SUBLANES, :] = ext_ref[tb:tb + SUBLANES, :]

    row = lax.broadcasted_iota(jnp.int32, (tb, D_LRU), 0) & (SUBLANES - 1)

    def shift_in_group(x, s):
        return pltpu.roll(x.reshape(tb // SUBLANES, SUBLANES, D_LRU), s, 1).reshape(tb, D_LRU)

    x = xl_ref[...]
    ext_ref[SUBLANES:tb + SUBLANES, :] = x
    x_up = ext_ref[0:tb, :]

    def delayed(s):
        return jnp.where(row >= s, shift_in_group(x, s), shift_in_group(x_up, s))

    wc = wc_ref[...]
    xc = bc_ref[...] + wc[0:1] * delayed(3)
    xc = xc + wc[1:2] * delayed(2)
    xc = xc + wc[2:3] * delayed(1)
    xc = xc + wc[3:4] * x

    a, b = _lru_gates(xc, wa_ref, wx_ref, ba_ref[...], bx_ref[...], lam_ref[...])

    for s in (1, 2, 4):
        a_sh = shift_in_group(a, s)
        b_sh = shift_in_group(b, s)
        take = row >= s
        b = jnp.where(take, a * b_sh + b, b)
        a = jnp.where(take, a * a_sh, a)

    carry = h_ref[...]
    for g in range(tb // SUBLANES):
        sl = slice(g * SUBLANES, (g + 1) * SUBLANES)
        hg = a[sl] * carry + b[sl]
        hs_ref[sl, :] = hg
        carry = jnp.broadcast_to(hg[SUBLANES - 1:SUBLANES, :], (SUBLANES, D_LRU))
    h_ref[...] = carry

    lru = _gelu_tanh(gl_ref[...]) * hs_ref[...]
    out_ref[...] = _rms(lru, g_ref[...]).astype(out_ref.dtype)

    @pl.when(c == pl.num_programs(1) - 1)
    def _():
        ht_ref[0] = carry[0:1]


def _lru(xl, gl, w_conv, b_conv, wa, wx, ba, bx, lam, g_lru, *, n_seq, seq, tb):
    nc = seq // tb
    row = lambda b, c: (b * nc + c, 0)
    const2 = lambda b, c: (0, 0)
    const3 = lambda b, c: (0, 0, 0)
    vec = pl.BlockSpec((1, D_LRU), const2)
    return pl.pallas_call(
        _lru_kernel,
        grid=(n_seq, nc),
        in_specs=[
            pl.BlockSpec((tb, D_LRU), row), pl.BlockSpec((tb, D_LRU), row),
            pl.BlockSpec((CONV_W, D_LRU), const2), vec,
            pl.BlockSpec((N_LRU_BLOCKS, LRU_BLOCK, LRU_BLOCK), const3),
            pl.BlockSpec((N_LRU_BLOCKS, LRU_BLOCK, LRU_BLOCK), const3),
            vec, vec, vec, vec,
        ],
        out_specs=(pl.BlockSpec((tb, D_LRU), row),
                   pl.BlockSpec((1, 1, D_LRU), lambda b, c: (b, 0, 0))),
        out_shape=(jax.ShapeDtypeStruct((n_seq * seq, D_LRU), bf16),
                   jax.ShapeDtypeStruct((n_seq, 1, D_LRU), f32)),
        scratch_shapes=[pltpu.VMEM((tb + SUBLANES, D_LRU), f32),
                        pltpu.VMEM((SUBLANES, D_LRU), f32),
                        pltpu.VMEM((tb, D_LRU), f32)],
        compiler_params=_params(("parallel", "arbitrary"), 40),
        name="lru",
    )(xl, gl, w_conv, b_conv.reshape(1, D_LRU), wa, wx, ba.reshape(1, D_LRU),
      bx.reshape(1, D_LRU), lam.reshape(1, D_LRU), g_lru.reshape(1, D_LRU))


def _lru_step_kernel(xl_ref, gl_ref, s0_ref, s1_ref, s2_ref, h0_ref, wc_ref, bc_ref, wa_ref, wx_ref,
                     ba_ref, bx_ref, lam_ref, g_ref, out_ref, h_ref):
    wc = wc_ref[...]
    xc = bc_ref[...] + wc[0:1] * s0_ref[...]
    xc = xc + wc[1:2] * s1_ref[...]
    xc = xc + wc[2:3] * s2_ref[...]
    xc = xc + wc[3:4] * xl_ref[...]
    a, b = _lru_gates(xc, wa_ref, wx_ref, ba_ref[...], bx_ref[...], lam_ref[...])
    h = a * h0_ref[...] + b
    h_ref[...] = h
    out_ref[...] = _rms(_gelu_tanh(gl_ref[...]) * h, g_ref[...]).astype(out_ref.dtype)


def _lru_step(xl, gl, s0, s1, s2, h0, w_conv, b_conv, wa, wx, ba, bx, lam, g_lru):
    n = xl.shape[0]
    return pl.pallas_call(
        _lru_step_kernel,
        out_shape=(jax.ShapeDtypeStruct((n, D_LRU), bf16), jax.ShapeDtypeStruct((n, D_LRU), f32)),
        name="lru_step",
    )(xl, gl, s0, s1, s2, h0, w_conv, b_conv.reshape(1, D_LRU), wa, wx, ba.reshape(1, D_LRU),
      bx.reshape(1, D_LRU), lam.reshape(1, D_LRU), g_lru.reshape(1, D_LRU))


def _attn_kernel(q_ref, k_ref, v_ref, lft_ref, g_ref, o_ref, nf_ref, att_ref, m_ref, l_ref, *, blk, kblk):
    qi = pl.program_id(1)
    n_kblk = nf_ref.shape[0]
    per_q = blk // kblk

    @pl.when(qi == 0)
    def _():
        r = lax.broadcasted_iota(jnp.int32, (kblk, 2 * kblk), 0)
        c = lax.broadcasted_iota(jnp.int32, (kblk, 2 * kblk), 1)
        tri_ones = jnp.where((r <= c) | (c >= kblk), 1.0, 0.0).astype(bf16)
        carry = jnp.zeros((F_ROWS, kblk), f32)
        for j in range(n_kblk):
            hi, mid, lo = _split3(lft_ref[:, j * kblk:(j + 1) * kblk])
            res = (jnp.dot(hi, tri_ones, preferred_element_type=f32)
                   + jnp.dot(mid, tri_ones, preferred_element_type=f32)
                   + jnp.dot(lo, tri_ones, preferred_element_type=f32))
            cum = carry + res[:, :kblk]
            nf_ref[j] = -cum[0:N_HEADS]
            carry = carry + res[:, kblk:]

    rows = lax.broadcasted_iota(jnp.int32, (blk, kblk), 0)
    colsi = lax.broadcasted_iota(jnp.int32, (blk, kblk), 1)

    m_ref[...] = jnp.full(m_ref.shape, NEG_INF, f32)
    l_ref[...] = jnp.zeros(l_ref.shape, f32)
    att_ref[...] = jnp.zeros(att_ref.shape, f32)

    def kv_block(j, diag):
        off = pl.multiple_of(j * kblk, kblk)
        nf = nf_ref[j]
        for h in range(N_HEADS):
            hs = slice(h * HEAD_DIM, (h + 1) * HEAD_DIM)
            s = lax.dot_general(q_ref[:, hs], k_ref[pl.ds(off, kblk), hs], (((1,), (1,)), ((), ())),
                                preferred_element_type=f32) + nf[h:h + 1, :]
            if diag is not None:
                s = jnp.where(colsi + diag * kblk <= rows, s, NEG_INF)
            m_old = m_ref[h]
            m_new = jnp.maximum(m_old, jnp.max(s, axis=-1, keepdims=True))
            alpha = jnp.exp(m_old - m_new)
            p = jnp.exp(s - jnp.concatenate([m_new] * (kblk // LANES), axis=1))
            m_ref[h] = m_new
            l_ref[h] = alpha * l_ref[h] + jnp.sum(p, axis=-1, keepdims=True)
            att_ref[:, hs] = alpha * att_ref[:, hs] + jnp.dot(
                p.astype(bf16), v_ref[pl.ds(off, kblk), hs], preferred_element_type=f32)

    def body(j, carry):
        kv_block(j, None)
        return carry

    lax.fori_loop(0, qi * per_q, body, 0)
    for d in range(per_q):
        kv_block(qi * per_q + d, d)
    for h in range(N_HEADS):
        hs = slice(h * HEAD_DIM, (h + 1) * HEAD_DIM)
        att_ref[:, hs] = att_ref[:, hs] / l_ref[h]
    o_ref[...] = _rms(att_ref[...], g_ref[...]).astype(o_ref.dtype)


def _attn(q, kb, vb, lft, g_att, *, n_seq, seq, blk, kblk):
    nq = seq // blk
    return pl.pallas_call(
        functools.partial(_attn_kernel, blk=blk, kblk=kblk),
        grid=(n_seq, nq),
        in_specs=[
            pl.BlockSpec((blk, D_ATT), lambda b, i: (b * nq + i, 0)),
            pl.BlockSpec((seq, D_ATT), lambda b, i: (b, 0)),
            pl.BlockSpec((seq, D_ATT), lambda b, i: (b, 0)),
            pl.BlockSpec((F_ROWS, seq), lambda b, i: (0, b)),
            pl.BlockSpec((1, D_ATT), lambda b, i: (0, 0)),
        ],
        out_specs=pl.BlockSpec((blk, D_ATT), lambda b, i: (b * nq + i, 0)),
        out_shape=jax.ShapeDtypeStruct((n_seq * seq, D_ATT), bf16),
        scratch_shapes=[pltpu.VMEM((seq // kblk, N_HEADS, kblk), f32), pltpu.VMEM((blk, D_ATT), f32),
                        pltpu.VMEM((N_HEADS, blk, LANES), f32), pltpu.VMEM((N_HEADS, blk, LANES), f32)],
        compiler_params=_params(("parallel", "arbitrary"), 40),
        name="attn",
    )(q, kb, vb, lft, g_att.reshape(1, D_ATT))


PAGES_PER_STEP = 2
RING_STEPS = 8
FETCH_AHEAD = RING_STEPS - 2
PAGE_SLOTS = RING_STEPS * PAGES_PER_STEP
N_PARTIAL = 4
TIE_LAG = 24


def _tree_sum(xs):
    while len(xs) > 1:
        xs = [a + b for a, b in zip(xs[0::2], xs[1::2])] + xs[len(xs) & ~1:]
    return xs[0]


def _lockstep(lead, follow):
    next(follow)
    lead_result = follow_result = None
    while True:
        try:
            fresh = next(lead)
        except StopIteration as done:
            lead_result = done.value
            break
        try:
            follow.send(fresh)
        except StopIteration as done:
            follow_result = done.value
    return lead_result, follow_result


def _alternate(*passes):
    results = [None] * len(passes)
    live = list(range(len(passes)))
    while live:
        for i in list(live):
            try:
                next(passes[i])
            except StopIteration as done:
                results[i] = done.value
                live.remove(i)
    return results


def _decode_kernel(pt_ref, q_ref, kn_ref, vn_ref, lfn_ref, g_ref, k_hbm, v_hbm, lf_hbm,
                   o_ref, kbuf, vbuf, lfbuf, sc_a, sc_b, tri_ref, onehot_ref, sem, *, n_pages):
    page = kbuf.shape[1]
    steps_per_seq = n_pages // PAGES_PER_STEP
    n_steps = o_ref.shape[0] * steps_per_seq
    last_of_seq = steps_per_seq - 1

    def page_copies(slot, pid):
        return (pltpu.make_async_copy(k_hbm.at[pid], kbuf.at[slot], sem.at[0, slot]),
                pltpu.make_async_copy(v_hbm.at[pid], vbuf.at[slot], sem.at[1, slot]),
                pltpu.make_async_copy(lf_hbm.at[pid], lfbuf.at[slot], sem.at[2, slot]))

    def slot_base(step):
        return (step % RING_STEPS) * PAGES_PER_STEP

    def start_fetch(step):
        for u in range(PAGES_PER_STEP):
            for cp in page_copies(slot_base(step) + u, pt_ref[step * PAGES_PER_STEP + u]):
                cp.start()

    def wait_fetch(step, streams):
        for u in range(PAGES_PER_STEP):
            copies = page_copies(slot_base(step) + u, 0)
            for stream in streams:
                copies[stream].wait()

    r = lax.broadcasted_iota(jnp.int32, (page, 2 * page), 0)
    c = lax.broadcasted_iota(jnp.int32, (page, 2 * page), 1)
    tri_ref[...] = jnp.where((r <= c) | (c >= page), 1.0, 0.0).astype(bf16)
    lane = lax.broadcasted_iota(jnp.int32, (N_HEADS, LANES), 1)
    for s in range(page):
        onehot_ref[s] = jnp.where(lane == s, 1.0, 0.0)
    never = pt_ref[0] < 0

    def page_forget(ring):
        base = ring * PAGES_PER_STEP
        lf2 = jnp.concatenate([lfbuf[base + u] for u in range(PAGES_PER_STEP)], axis=0)
        hi, mid, lo = _split3(lf2)
        tri_ones = tri_ref[...]
        return (jnp.dot(hi, tri_ones, preferred_element_type=f32)
                + jnp.dot(mid, tri_ones, preferred_element_type=f32)
                + jnp.dot(lo, tri_ones, preferred_element_type=f32)) * LOG2E

    def score_pass(step, ring, car, res, sc_ref):
        q = q_ref[step // steps_per_seq] * LOG2E
        base = ring * PAGES_PER_STEP
        car = jnp.where(step % steps_per_seq == 0, 0.0, car)
        cars, m_loc = [], []
        for u in range(PAGES_PER_STEP):
            rows = slice(u * N_HEADS, (u + 1) * N_HEADS)
            cum = res[rows, :page]
            part = [None] * N_PARTIAL
            for s in range(page):
                t = kbuf[base + u, s] * q - cum * onehot_ref[s]
                sc = jnp.broadcast_to(jnp.sum(t, axis=-1, keepdims=True), (N_HEADS, LANES))
                sc_ref[u, s] = sc
                i = s % N_PARTIAL
                part[i] = sc if part[i] is None else jnp.maximum(part[i], sc)
                yield sc
            cars.append(car)
            m_loc.append(functools.reduce(jnp.maximum, part) - car)
            car = car + res[rows, page:]
        return tuple(m_loc), tuple(cars), car

    def value_pass(step, ring, soft, scored, sc_ref, tie_ref):
        m_loc, cars, _ = scored
        base = ring * PAGES_PER_STEP
        first = step % steps_per_seq == 0
        m_run = jnp.where(first, NEG_INF, soft[0])
        l_run = jnp.where(first, 0.0, soft[1])
        a_run = jnp.where(first, 0.0, soft[2])
        m_new = functools.reduce(jnp.maximum, list(m_loc) + [m_run])
        alpha = jnp.exp2(m_run - m_new)
        l_parts, a_parts = [l_run * alpha], [a_run * alpha]
        recent = []
        for u in range(PAGES_PER_STEP):
            shift = m_new + cars[u]
            lp = [None] * N_PARTIAL
            ap = [None] * N_PARTIAL
            for s in range(page):
                recent.append((yield))
                logit = sc_ref[u, s]
                if len(recent) > TIE_LAG:
                    logit = jnp.where(never, recent.pop(0), logit)
                pr = jnp.exp2(logit - shift)
                pv = pr * vbuf[base + u, s]
                i = s % N_PARTIAL
                lp[i] = pr if lp[i] is None else lp[i] + pr
                ap[i] = pv if ap[i] is None else ap[i] + pv
            l_parts += lp
            a_parts += ap
        return m_new, _tree_sum(l_parts), _tree_sum(a_parts)

    def write_output(step, soft, car_end):
        b = step // steps_per_seq
        m_run, l_run, a_run = soft

        @pl.when(step % steps_per_seq == last_of_seq)
        def _():
            q = q_ref[b] * LOG2E
            s_new = jnp.sum(kn_ref[b] * q, axis=-1, keepdims=True) - (car_end + lfn_ref[b] * LOG2E)
            m_all = jnp.maximum(m_run, s_new)
            w = jnp.exp2(m_run - m_all)
            w_new = jnp.exp2(s_new - m_all)
            o = (a_run * w + w_new * vn_ref[b]) / (l_run * w + w_new)
            ms = jnp.sum(jnp.sum(o * o, axis=-1, keepdims=True), axis=0, keepdims=True) / D_ATT
            o_ref[b] = o * lax.rsqrt(ms + EPS) * g_ref[...]

    def half(step, ring, soft, scored, res_nxt):
        nxt = step + 1
        sc_cur, sc_nxt = (sc_a, sc_b) if ring % 2 == 0 else (sc_b, sc_a)

        @pl.when(nxt + FETCH_AHEAD < n_steps)
        def _():
            start_fetch(nxt + FETCH_AHEAD)

        @pl.when(nxt < n_steps)
        def _():
            wait_fetch(nxt, (0, 1))

        @pl.when(nxt + 1 < n_steps)
        def _():
            wait_fetch(nxt + 1, (2,))

        res_after = page_forget((ring + 2) % RING_STEPS)
        scored_nxt, soft = _lockstep(
            score_pass(jnp.minimum(nxt, n_steps - 1), (ring + 1) % RING_STEPS, scored[2], res_nxt, sc_nxt),
            value_pass(step, ring, soft, scored, sc_cur, sc_nxt))
        write_output(step, soft, scored[2])
        return soft, scored_nxt, res_after

    assert n_steps > FETCH_AHEAD
    for step in range(FETCH_AHEAD + 1):
        start_fetch(step)
    wait_fetch(0, (0, 1, 2))
    wait_fetch(1, (2,))
    zero = jnp.zeros((N_HEADS, LANES), f32)
    scored0, = _alternate(score_pass(0, 0, zero, page_forget(0), sc_a))

    def trip(t, carry):
        soft, scored, res_nxt = carry
        for ring in range(RING_STEPS):
            soft, scored, res_nxt = half(t * RING_STEPS + ring, ring, soft, scored, res_nxt)
        return soft, scored, res_nxt

    lax.fori_loop(0, n_steps // RING_STEPS, trip, ((zero, zero, zero), scored0, page_forget(1)))


def _decode(page_table, q, k_new, v_new, lf_new, cache_k, cache_v, cache_lf_t, g_att):
    n_seq, n_pages = page_table.shape
    page = cache_k.shape[1]
    assert n_pages % PAGES_PER_STEP == 0 and page == LANES
    assert (n_seq * n_pages // PAGES_PER_STEP) % RING_STEPS == 0
    pt = page_table.reshape(-1)
    tok = pl.BlockSpec((n_seq, N_HEADS, HEAD_DIM), lambda i, pt: (0, 0, 0))
    hbm = pl.BlockSpec(memory_space=pl.ANY)
    grid_spec = pltpu.PrefetchScalarGridSpec(
        num_scalar_prefetch=1,
        grid=(1,),
        in_specs=[tok, tok, tok, tok, pl.BlockSpec((N_HEADS, HEAD_DIM), lambda i, pt: (0, 0)),
                  hbm, hbm, hbm],
        out_specs=tok,
        scratch_shapes=[pltpu.VMEM((PAGE_SLOTS, page, N_HEADS, HEAD_DIM), f32),
                        pltpu.VMEM((PAGE_SLOTS, page, N_HEADS, HEAD_DIM), f32),
                        pltpu.VMEM((PAGE_SLOTS, N_HEADS, page), f32),
                        pltpu.VMEM((PAGES_PER_STEP, page, N_HEADS, LANES), f32),
                        pltpu.VMEM((PAGES_PER_STEP, page, N_HEADS, LANES), f32),
                        pltpu.VMEM((page, 2 * page), bf16),
                        pltpu.VMEM((page, N_HEADS, LANES), f32),
                        pltpu.SemaphoreType.DMA((3, PAGE_SLOTS))],
    )
    return pl.pallas_call(
        functools.partial(_decode_kernel, n_pages=n_pages),
        grid_spec=grid_spec,
        out_shape=jax.ShapeDtypeStruct((n_seq, N_HEADS, HEAD_DIM), f32),
        compiler_params=_params(("arbitrary",), 32),
        name="decode",
    )(pt, q, k_new, v_new, lf_new, g_att.reshape(N_HEADS, HEAD_DIM), cache_k, cache_v, cache_lf_t)


def _out_proj_kernel(att_ref, lru_ref, x_ref, w_ref, g_ref, x1_ref, xn_ref):
    mix = jnp.concatenate([att_ref[...].astype(bf16), lru_ref[...].astype(bf16)], axis=1)
    x1 = x_ref[...] + jnp.dot(mix, w_ref[...], preferred_element_type=f32)
    x1_ref[...] = x1
    xn_ref[...] = _rms(x1, g_ref[...]).astype(bf16)


def _out_proj(att_n, lru_n, x, w_out, g_ffn, *, tm):
    m = x.shape[0]
    row = lambda i: (i, 0)
    const = lambda i: (0, 0)
    return pl.pallas_call(
        _out_proj_kernel,
        grid=(m // tm,),
        in_specs=[pl.BlockSpec((tm, D_ATT), row), pl.BlockSpec((tm, D_LRU), row),
                  pl.BlockSpec((tm, D_MODEL), row),
                  pl.BlockSpec((D_MODEL, D_MODEL), const, pipeline_mode=pl.Buffered(1)),
                  pl.BlockSpec((1, D_MODEL), const)],
        out_specs=(pl.BlockSpec((tm, D_MODEL), row), pl.BlockSpec((tm, D_MODEL), row)),
        out_shape=(jax.ShapeDtypeStruct((m, D_MODEL), f32), jax.ShapeDtypeStruct((m, D_MODEL), bf16)),
        compiler_params=_params(("parallel",), 48),
        name="out_proj",
    )(att_n, lru_n, x, w_out, g_ffn.reshape(1, D_MODEL))


def _ffn_kernel(xn_ref, x1_ref, xns_ref, x1s_ref, wu_ref, wd_ref, g_ref, o_ref, os_ref):
    i = pl.program_id(0)
    f = pl.program_id(1)
    last_i = pl.num_programs(0) - 1
    last_f = pl.num_programs(1) - 1
    tm = xn_ref.shape[0]

    def mlp(x):
        hf = jnp.dot(x, wu_ref[...].astype(bf16), preferred_element_type=f32)
        return jnp.dot(jnp.square(jnp.maximum(hf, 0.0)).astype(bf16), wd_ref[...].astype(bf16),
                       preferred_element_type=f32)

    @pl.when(f == 0)
    def _():
        o_ref[...] = x1_ref[...]

    @pl.when(i < last_i)
    def _():
        o_ref[...] += mlp(xn_ref[...])

    @pl.when(i == last_i)
    def _():
        @pl.when(f == 0)
        def _():
            os_ref[...] = x1s_ref[...]

        y = mlp(jnp.concatenate([xn_ref[...], xns_ref[...]], axis=0))
        o_ref[...] += y[:tm]
        os_ref[...] += y[tm:]

        @pl.when(f == last_f)
        def _():
            os_ref[...] = _rms(os_ref[...], g_ref[...])

    @pl.when(f == last_f)
    def _():
        o_ref[...] = _rms(o_ref[...], g_ref[...])


def _ffn(xn, x1, xn_s, x1_s, w_up, w_down, g_final, *, tm, tf):
    m, ms = x1.shape[0], x1_s.shape[0]
    row = lambda i, f: (i, 0)
    const = lambda i, f: (0, 0)
    return pl.pallas_call(
        _ffn_kernel,
        grid=(m // tm, D_FF // tf),
        in_specs=[pl.BlockSpec((tm, D_MODEL), row, pipeline_mode=pl.Buffered(1)),
                  pl.BlockSpec((tm, D_MODEL), row, pipeline_mode=pl.Buffered(1)),
                  pl.BlockSpec((ms, D_MODEL), const), pl.BlockSpec((ms, D_MODEL), const),
                  pl.BlockSpec((D_MODEL, tf), lambda i, f: (0, f)),
                  pl.BlockSpec((tf, D_MODEL), lambda i, f: (f, 0)),
                  pl.BlockSpec((1, D_MODEL), const)],
        out_specs=(pl.BlockSpec((tm, D_MODEL), row), pl.BlockSpec((ms, D_MODEL), const)),
        out_shape=(jax.ShapeDtypeStruct((m, D_MODEL), f32), jax.ShapeDtypeStruct((ms, D_MODEL), f32)),
        compiler_params=_params(("arbitrary", "arbitrary"), 56),
        name="ffn",
    )(xn, x1, xn_s, x1_s, w_up, w_down, g_final.reshape(1, D_MODEL))


def kernel(x_prompt, x_sample, cache_k, cache_v, cache_logf, state_conv, state_h, page_table,
           g_mix, w_in, b_f, w_conv, b_conv, w_gate_a, b_gate_a, w_gate_x, b_gate_x,
           lru_lambda, g_att_out, g_lru_out, w_out, g_ffn, w_up, w_down, g_final):
    depth = w_in.shape[0]
    n_seq, seq, _ = x_prompt.shape
    n_dec = x_sample.shape[0]
    assert depth == 1 and x_sample.shape[1] == 1

    l = 0
    c_f = 3 * D_ATT
    w_qkv = w_in[l][:, :c_f].astype(bf16)
    w_lru = w_in[l][:, c_f + N_HEADS:].astype(bf16)
    wft = jnp.pad(w_in[l][:, c_f:c_f + N_HEADS].T, ((0, F_ROWS - N_HEADS), (0, 0))).astype(bf16)
    wa = w_gate_a[l].astype(bf16)
    wx = w_gate_x[l].astype(bf16)
    wo = w_out[l].astype(bf16)
    wu = w_up[l]
    wd = w_down[l]
    lru_w = (w_conv[l], b_conv[l], wa, wx, b_gate_a[l], b_gate_x[l], lru_lambda[l], g_lru_out[l])

    xp = x_prompt.reshape(n_seq * seq, D_MODEL)
    xs = x_sample.reshape(n_dec, D_MODEL)
    (q, k, v, kb, vb, xl, gl, lft), (qs, ks, vs, xls, gls, lfts) = _in_proj(
        xp, xs, g_mix[l], w_qkv, w_lru, wft, b_f[l], tm=256)
    lru_n, h_t = _lru(xl, gl, *lru_w, n_seq=n_seq, seq=seq, tb=256)
    att_n = _attn(q, kb, vb, lft, g_att_out[l], n_seq=n_seq, seq=seq, blk=256, kblk=256)
    x1, xn = _out_proj(att_n, lru_n, xp, wo, g_ffn[l], tm=512)

    new_k_p = k.reshape(1, n_seq, seq, N_HEADS, HEAD_DIM)
    new_v_p = v.reshape(1, n_seq, seq, N_HEADS, HEAD_DIM)
    new_f_p = lft[:N_HEADS].T.reshape(1, n_seq, seq, N_HEADS)
    new_c_p = xl.reshape(n_seq, seq, D_LRU)[:, seq - (CONV_W - 1):, :][None]
    new_h_p = h_t.reshape(1, n_seq, D_LRU)

    lf_s = lfts[:N_HEADS].T
    heads = lambda t: t.reshape(n_dec, N_HEADS, HEAD_DIM)
    lf_wide = jnp.broadcast_to(lf_s[:, :, None], (n_dec, N_HEADS, HEAD_DIM))
    att_s = _decode(page_table, heads(qs), heads(ks), heads(vs), lf_wide, cache_k[l], cache_v[l],
                    jnp.swapaxes(cache_logf[l], 1, 2), g_att_out[l]).reshape(n_dec, D_ATT)
    sc = state_conv[l]
    lru_s, h_s = _lru_step(xls, gls, sc[:, 0], sc[:, 1], sc[:, 2], state_h[l], *lru_w)
    x1s, xns = _out_proj(att_s, lru_s, xs, wo, g_ffn[l], tm=n_dec)

    y_prompt, y_sample = _ffn(xn, x1, xns, x1s, wu, wd, g_final, tm=1024, tf=512)
    y_prompt = y_prompt.reshape(n_seq, seq, D_MODEL)
    y_sample = y_sample.reshape(n_dec, 1, D_MODEL)

    new_k_s = ks.reshape(1, n_dec, 1, N_HEADS, HEAD_DIM)
    new_v_s = vs.reshape(1, n_dec, 1, N_HEADS, HEAD_DIM)
    new_f_s = lf_s.reshape(1, n_dec, 1, N_HEADS)
    new_c_s = jnp.stack([sc[:, 1], sc[:, 2], xls], axis=1)[None]
    new_h_s = h_s[None]

    return (y_prompt, y_sample, new_k_p, new_v_p, new_f_p, new_c_p, new_h_p,
            new_k_s, new_v_s, new_f_s, new_c_s, new_h_s)
```

```python
import functools

import jax
import jax.numpy as jnp
from jax import lax
from jax.experimental import pallas as pl
from jax.experimental.pallas import tpu as pltpu

D_MODEL = 2048
D_ATT = 1024
D_LRU = 1024
HEAD_DIM = 128
N_HEADS = 8
N_LRU_BLOCKS = 8
LRU_BLOCK = 128
CONV_W = 4
LRU_C = 8.0
D_FF = 4 * D_MODEL
EPS = 1e-6
NEG_INF = -1e30
LOG2E = 1.4426950408889634
F_ROWS = 16
W_TAIL_COLS = 17 * 128
SUBLANES = 8
LANES = 128
MIB = 1024 * 1024

f32 = jnp.float32
bf16 = jnp.bfloat16


def _rms(x, g):
    return x * lax.rsqrt(jnp.mean(x * x, axis=-1, keepdims=True) + EPS) * g


def _softplus(x):
    return jnp.maximum(x, 0.0) + jnp.log1p(jnp.exp(-jnp.abs(x)))


def _sigmoid(x):
    return 0.5 * jnp.tanh(0.5 * x) + 0.5


def _gelu_tanh(x):
    c = 0.7978845608028654
    return x * (0.5 * (1.0 + jnp.tanh(c * (x + 0.044715 * (x * x * x)))))


def _split3(x):
    hi = x.astype(bf16)
    r1 = x - hi.astype(f32)
    mid = r1.astype(bf16)
    lo = (r1 - mid.astype(f32)).astype(bf16)
    return hi, mid, lo


def _params(sem, vmem_mib, flags=None):
    return pltpu.CompilerParams(dimension_semantics=sem, vmem_limit_bytes=vmem_mib * MIB, flags=flags)


def _in_proj_kernel(x_ref, xs_ref, g_ref, wt_ref, bf_ref,
                    q_ref, k_ref, v_ref, kb_ref, vb_ref, xl_ref, gl_ref, lft_ref,
                    qs_ref, ks_ref, vs_ref, xls_ref, gls_ref, lfts_ref):
    i = pl.program_id(0)
    last = pl.num_programs(0) - 1
    tm = x_ref.shape[0]

    def project(x):
        xn = _rms(x, g_ref[...]).astype(bf16)

        def cols(c0, width):
            return lax.dot_general(xn, wt_ref[c0:c0 + width, :], (((1,), (1,)), ((), ())),
                                   preferred_element_type=f32)

        c_f = 3 * D_ATT
        t = lax.dot_general(wt_ref[c_f:c_f + F_ROWS, :], xn, (((1,), (1,)), ((), ())),
                            preferred_element_type=f32) + bf_ref[:, :x.shape[0]]
        lf = jnp.minimum(t, 0.0) - jnp.log1p(jnp.exp(-jnp.abs(t)))
        tail = cols(c_f, W_TAIL_COLS)
        xl = tail[:, N_HEADS:N_HEADS + D_LRU]
        gl = tail[:, N_HEADS + D_LRU:N_HEADS + 2 * D_LRU]
        return (cols(0, D_ATT) * (HEAD_DIM ** -0.5 * LOG2E), cols(D_ATT, D_ATT), cols(2 * D_ATT, D_ATT),
                xl, gl, lf)

    def store_prompt(q, k, v, xl, gl, lf):
        q_ref[...] = q.astype(q_ref.dtype)
        k_ref[...] = k
        kb_ref[...] = k.astype(bf16)
        v_ref[...] = v
        vb_ref[...] = v.astype(bf16)
        xl_ref[...] = xl
        gl_ref[...] = gl
        lft_ref[...] = lf

    @pl.when(i < last)
    def _():
        store_prompt(*project(x_ref[...]))

    @pl.when(i == last)
    def _():
        q, k, v, xl, gl, lf = project(jnp.concatenate([x_ref[...], xs_ref[...]], axis=0))
        store_prompt(q[:tm], k[:tm], v[:tm], xl[:tm], gl[:tm], lf[:, :tm])
        qs_ref[...] = q[tm:]
        ks_ref[...] = k[tm:]
        vs_ref[...] = v[tm:]
        xls_ref[...] = xl[tm:]
        gls_ref[...] = gl[tm:]
        lfts_ref[...] = lf[:, tm:]


def _in_proj(x, x_s, g_mix, w_t, b_f, *, tm):
    m, ms = x.shape[0], x_s.shape[0]
    bfb = jnp.broadcast_to(jnp.pad(b_f, (0, F_ROWS - N_HEADS))[:, None], (F_ROWS, tm + ms)).astype(f32)
    row = lambda i: (i, 0)
    const = lambda i: (0, 0)
    wide = pl.BlockSpec((tm, D_ATT), row)
    small = pl.BlockSpec((ms, D_ATT), const)
    wide_shape = lambda dt: jax.ShapeDtypeStruct((m, D_ATT), dt)
    small_shape = jax.ShapeDtypeStruct((ms, D_ATT), f32)
    outs = pl.pallas_call(
        _in_proj_kernel,
        grid=(m // tm,),
        in_specs=[
            pl.BlockSpec((tm, D_MODEL), row),
            pl.BlockSpec((ms, D_MODEL), const),
            pl.BlockSpec((1, D_MODEL), const),
            pl.BlockSpec((3 * D_ATT + W_TAIL_COLS, D_MODEL), const, pipeline_mode=pl.Buffered(1)),
            pl.BlockSpec((F_ROWS, tm + ms), const),
        ],
        out_specs=(wide,) * 7 + (pl.BlockSpec((F_ROWS, tm), lambda i: (0, i)),)
        + (small,) * 5 + (pl.BlockSpec((F_ROWS, ms), const),),
        out_shape=(wide_shape(bf16), wide_shape(f32), wide_shape(f32), wide_shape(bf16), wide_shape(bf16),
                   wide_shape(f32), wide_shape(f32), jax.ShapeDtypeStruct((F_ROWS, m), f32))
        + (small_shape,) * 5 + (jax.ShapeDtypeStruct((F_ROWS, ms), f32),),
        compiler_params=_params(("arbitrary",), 52),
        name="in_proj",
    )(x, x_s, g_mix.reshape(1, D_MODEL), w_t, bfb)
    return outs[:8], outs[8:]


def _lru_gates(xc, wa_ref, wx_ref, ba, bx, lam):
    xcb = xc.astype(bf16)
    r_parts, i_parts = [], []
    for n in range(N_LRU_BLOCKS):
        blk = xcb[:, n * LRU_BLOCK:(n + 1) * LRU_BLOCK]
        r_parts.append(jnp.dot(blk, wa_ref[n], preferred_element_type=f32))
        i_parts.append(jnp.dot(blk, wx_ref[n], preferred_element_type=f32))
    r = _sigmoid(jnp.concatenate(r_parts, axis=1) + ba)
    i = _sigmoid(jnp.concatenate(i_parts, axis=1) + bx)
    log_a = (-LRU_C * r) * _softplus(-lam)
    a = jnp.exp(log_a)
    th = jnp.tanh(log_a)
    u = -2.0 * th
    b = jnp.where(u > 0.0, u * lax.rsqrt(u * (1.0 - th)), 0.0) * i * xc
    return a, b


def _lru_kernel(xl_ref, gl_ref, wc_ref, bc_ref, wa_ref, wx_ref, ba_ref, bx_ref, lam_ref, g_ref,
                out_ref, ht_ref, ext_ref, h_ref, hs_ref):
    c = pl.program_id(1)
    tb = xl_ref.shape[0]

    @pl.when(c == 0)
    def _():
        ext_ref[0:SUBLANES, :] = jnp.zeros((SUBLANES, D_LRU), f32)
        h_ref[...] = jnp.zeros((SUBLANES, D_LRU), f32)

    @pl.when(c > 0)
    def _():
        ext_ref[0:SUBLANES, :] = ext_ref[tb:tb + SUBLANES, :]

    row = lax.broadcasted_iota(jnp.int32, (tb, D_LRU), 0) & (SUBLANES - 1)

    def shift_in_group(x, s):
        return pltpu.roll(x.reshape(tb // SUBLANES, SUBLANES, D_LRU), s, 1).reshape(tb, D_LRU)

    x = xl_ref[...]
    ext_ref[SUBLANES:tb + SUBLANES, :] = x
    x_up = ext_ref[0:tb, :]

    def delayed(s):
        return jnp.where(row >= s, shift_in_group(x, s), shift_in_group(x_up, s))

    wc = wc_ref[...]
    xc = bc_ref[...] + wc[0:1] * delayed(3)
    xc = xc + wc[1:2] * delayed(2)
    xc = xc + wc[2:3] * delayed(1)
    xc = xc + wc[3:4] * x

    a, b = _lru_gates(xc, wa_ref, wx_ref, ba_ref[...], bx_ref[...], lam_ref[...])

    for s in (1, 2, 4):
        a_sh = shift_in_group(a, s)
        b_sh = shift_in_group(b, s)
        take = row >= s
        b = jnp.where(take, a * b_sh + b, b)
        a = jnp.where(take, a * a_sh, a)

    carry = h_ref[...]
    for g in range(tb // SUBLANES):
        sl = slice(g * SUBLANES, (g + 1) * SUBLANES)
        hg = a[sl] * carry + b[sl]
        hs_ref[sl, :] = hg
        carry = jnp.broadcast_to(hg[SUBLANES - 1:SUBLANES, :], (SUBLANES, D_LRU))
    h_ref[...] = carry

    lru = _gelu_tanh(gl_ref[...]) * hs_ref[...]
    out_ref[...] = _rms(lru, g_ref[...]).astype(out_ref.dtype)

    @pl.when(c == pl.num_programs(1) - 1)
    def _():
        ht_ref[0] = carry[0:1]


def _lru(xl, gl, w_conv, b_conv, wa, wx, ba, bx, lam, g_lru, *, n_seq, seq, tb):
    nc = seq // tb
    row = lambda b, c: (b * nc + c, 0)
    const2 = lambda b, c: (0, 0)
    const3 = lambda b, c: (0, 0, 0)
    vec = pl.BlockSpec((1, D_LRU), const2)
    return pl.pallas_call(
        _lru_kernel,
        grid=(n_seq, nc),
        in_specs=[
            pl.BlockSpec((tb, D_LRU), row), pl.BlockSpec((tb, D_LRU), row),
            pl.BlockSpec((CONV_W, D_LRU), const2), vec,
            pl.BlockSpec((N_LRU_BLOCKS, LRU_BLOCK, LRU_BLOCK), const3),
            pl.BlockSpec((N_LRU_BLOCKS, LRU_BLOCK, LRU_BLOCK), const3),
            vec, vec, vec, vec,
        ],
        out_specs=(pl.BlockSpec((tb, D_LRU), row),
                   pl.BlockSpec((1, 1, D_LRU), lambda b, c: (b, 0, 0))),
        out_shape=(jax.ShapeDtypeStruct((n_seq * seq, D_LRU), bf16),
                   jax.ShapeDtypeStruct((n_seq, 1, D_LRU), f32)),
        scratch_shapes=[pltpu.VMEM((tb + SUBLANES, D_LRU), f32),
                        pltpu.VMEM((SUBLANES, D_LRU), f32),
                        pltpu.VMEM((tb, D_LRU), f32)],
        compiler_params=_params(("parallel", "arbitrary"), 40),
        name="lru",
    )(xl, gl, w_conv, b_conv.reshape(1, D_LRU), wa, wx, ba.reshape(1, D_LRU),
      bx.reshape(1, D_LRU), lam.reshape(1, D_LRU), g_lru.reshape(1, D_LRU))


def _lru_step_kernel(xl_ref, gl_ref, s0_ref, s1_ref, s2_ref, h0_ref, wc_ref, bc_ref, wa_ref, wx_ref,
                     ba_ref, bx_ref, lam_ref, g_ref, out_ref, h_ref):
    wc = wc_ref[...]
    xc = bc_ref[...] + wc[0:1] * s0_ref[...]
    xc = xc + wc[1:2] * s1_ref[...]
    xc = xc + wc[2:3] * s2_ref[...]
    xc = xc + wc[3:4] * xl_ref[...]
    a, b = _lru_gates(xc, wa_ref, wx_ref, ba_ref[...], bx_ref[...], lam_ref[...])
    h = a * h0_ref[...] + b
    h_ref[...] = h
    out_ref[...] = _rms(_gelu_tanh(gl_ref[...]) * h, g_ref[...]).astype(out_ref.dtype)


def _lru_step(xl, gl, s0, s1, s2, h0, w_conv, b_conv, wa, wx, ba, bx, lam, g_lru):
    n = xl.shape[0]
    return pl.pallas_call(
        _lru_step_kernel,
        out_shape=(jax.ShapeDtypeStruct((n, D_LRU), bf16), jax.ShapeDtypeStruct((n, D_LRU), f32)),
        name="lru_step",
    )(xl, gl, s0, s1, s2, h0, w_conv, b_conv.reshape(1, D_LRU), wa, wx, ba.reshape(1, D_LRU),
      bx.reshape(1, D_LRU), lam.reshape(1, D_LRU), g_lru.reshape(1, D_LRU))


def _attn_kernel(q_ref, k_ref, v_ref, lft_ref, g_ref, o_ref, nf_ref, att_ref, m_ref, l_ref, *, blk, kblk):
    qi = pl.program_id(1)
    n_kblk = nf_ref.shape[0]
    per_q = blk // kblk

    @pl.when(qi == 0)
    def _():
        r = lax.broadcasted_iota(jnp.int32, (kblk, 2 * kblk), 0)
        c = lax.broadcasted_iota(jnp.int32, (kblk, 2 * kblk), 1)
        tri_ones = jnp.where((r <= c) | (c >= kblk), 1.0, 0.0).astype(bf16)
        carry = jnp.zeros((F_ROWS, kblk), f32)
        for j in range(n_kblk):
            hi, mid, lo = _split3(lft_ref[:, j * kblk:(j + 1) * kblk])
            res = (jnp.dot(hi, tri_ones, preferred_element_type=f32)
                   + jnp.dot(mid, tri_ones, preferred_element_type=f32)
                   + jnp.dot(lo, tri_ones, preferred_element_type=f32))
            cum = carry + res[:, :kblk]
            nf_ref[j] = -LOG2E * cum[0:N_HEADS]
            carry = carry + res[:, kblk:]

    rows = lax.broadcasted_iota(jnp.int32, (blk, kblk), 0)
    colsi = lax.broadcasted_iota(jnp.int32, (blk, kblk), 1)

    m_ref[...] = jnp.full(m_ref.shape, NEG_INF, f32)
    l_ref[...] = jnp.zeros(l_ref.shape, f32)
    att_ref[...] = jnp.zeros(att_ref.shape, f32)

    def kv_block(j, diag):
        off = pl.multiple_of(j * kblk, kblk)
        nf = nf_ref[j]
        for h in range(N_HEADS):
            hs = slice(h * HEAD_DIM, (h + 1) * HEAD_DIM)
            s = lax.dot_general(q_ref[:, hs], k_ref[pl.ds(off, kblk), hs], (((1,), (1,)), ((), ())),
                                preferred_element_type=f32) + nf[h:h + 1, :]
            if diag is not None:
                s = jnp.where(colsi + diag * kblk <= rows, s, NEG_INF)
            m_old = m_ref[h]
            m_new = jnp.maximum(m_old, jnp.max(s, axis=-1, keepdims=True))
            alpha = jnp.exp2(m_old - m_new)
            p = jnp.exp2(s - jnp.concatenate([m_new] * (kblk // LANES), axis=1))
            m_ref[h] = m_new
            l_ref[h] = alpha * l_ref[h] + jnp.sum(p, axis=-1, keepdims=True)
            att_ref[:, hs] = alpha * att_ref[:, hs] + jnp.dot(
                p.astype(bf16), v_ref[pl.ds(off, kblk), hs], preferred_element_type=f32)

    def body(j, carry):
        kv_block(j, None)
        return carry

    lax.fori_loop(0, qi * per_q, body, 0)
    for d in range(per_q):
        kv_block(qi * per_q + d, d)
    for h in range(N_HEADS):
        hs = slice(h * HEAD_DIM, (h + 1) * HEAD_DIM)
        att_ref[:, hs] = att_ref[:, hs] / l_ref[h]
    o_ref[...] = _rms(att_ref[...], g_ref[...]).astype(o_ref.dtype)


def _attn(q, kb, vb, lft, g_att, *, n_seq, seq, blk, kblk):
    nq = seq // blk
    return pl.pallas_call(
        functools.partial(_attn_kernel, blk=blk, kblk=kblk),
        grid=(n_seq, nq),
        in_specs=[
            pl.BlockSpec((blk, D_ATT), lambda b, i: (b * nq + i, 0)),
            pl.BlockSpec((seq, D_ATT), lambda b, i: (b, 0)),
            pl.BlockSpec((seq, D_ATT), lambda b, i: (b, 0)),
            pl.BlockSpec((F_ROWS, seq), lambda b, i: (0, b)),
            pl.BlockSpec((1, D_ATT), lambda b, i: (0, 0)),
        ],
        out_specs=pl.BlockSpec((blk, D_ATT), lambda b, i: (b * nq + i, 0)),
        out_shape=jax.ShapeDtypeStruct((n_seq * seq, D_ATT), bf16),
        scratch_shapes=[pltpu.VMEM((seq // kblk, N_HEADS, kblk), f32), pltpu.VMEM((blk, D_ATT), f32),
                        pltpu.VMEM((N_HEADS, blk, LANES), f32), pltpu.VMEM((N_HEADS, blk, LANES), f32)],
        compiler_params=_params(("parallel", "arbitrary"), 40),
        name="attn",
    )(q, kb, vb, lft, g_att.reshape(1, D_ATT))


PAGES_PER_STEP = 2
RING_STEPS = 8
FETCH_AHEAD = RING_STEPS - 2
PAGE_SLOTS = RING_STEPS * PAGES_PER_STEP
N_PARTIAL = 4
TIE_LAG = 24


def _tree_sum(xs):
    while len(xs) > 1:
        xs = [a + b for a, b in zip(xs[0::2], xs[1::2])] + xs[len(xs) & ~1:]
    return xs[0]


def _lockstep(lead, follow):
    next(follow)
    lead_result = follow_result = None
    while True:
        try:
            fresh = next(lead)
        except StopIteration as done:
            lead_result = done.value
            break
        try:
            follow.send(fresh)
        except StopIteration as done:
            follow_result = done.value
    return lead_result, follow_result


def _alternate(*passes):
    results = [None] * len(passes)
    live = list(range(len(passes)))
    while live:
        for i in list(live):
            try:
                next(passes[i])
            except StopIteration as done:
                results[i] = done.value
                live.remove(i)
    return results


def _decode_kernel(pt_ref, q_ref, kn_ref, vn_ref, lfn_ref, g_ref, k_hbm, v_hbm, lf_hbm,
                   o_ref, kbuf, vbuf, lfbuf, sc_a, sc_b, tri_ref, onehot_ref, sem, *, n_pages):
    page = kbuf.shape[1]
    steps_per_seq = n_pages // PAGES_PER_STEP
    n_steps = o_ref.shape[0] * steps_per_seq
    last_of_seq = steps_per_seq - 1

    def page_copies(slot, pid):
        return (pltpu.make_async_copy(k_hbm.at[pid], kbuf.at[slot], sem.at[0, slot]),
                pltpu.make_async_copy(v_hbm.at[pid], vbuf.at[slot], sem.at[1, slot]),
                pltpu.make_async_copy(lf_hbm.at[pid], lfbuf.at[slot], sem.at[2, slot]))

    def slot_base(step):
        return (step % RING_STEPS) * PAGES_PER_STEP

    def start_fetch(step):
        for u in range(PAGES_PER_STEP):
            for cp in page_copies(slot_base(step) + u, pt_ref[step * PAGES_PER_STEP + u]):
                cp.start()

    def wait_fetch(step, streams):
        for u in range(PAGES_PER_STEP):
            copies = page_copies(slot_base(step) + u, 0)
            for stream in streams:
                copies[stream].wait()

    r = lax.broadcasted_iota(jnp.int32, (page, 2 * page), 0)
    c = lax.broadcasted_iota(jnp.int32, (page, 2 * page), 1)
    tri_ref[...] = jnp.where((r <= c) | (c >= page), 1.0, 0.0).astype(bf16)
    lane = lax.broadcasted_iota(jnp.int32, (N_HEADS, LANES), 1)
    for s in range(page):
        onehot_ref[s] = jnp.where(lane == s, 1.0, 0.0)
    never = pt_ref[0] < 0

    def page_forget(ring):
        base = ring * PAGES_PER_STEP
        lf2 = jnp.concatenate([lfbuf[base + u] for u in range(PAGES_PER_STEP)], axis=0)
        hi, mid, lo = _split3(lf2)
        tri_ones = tri_ref[...]
        return (jnp.dot(hi, tri_ones, preferred_element_type=f32)
                + jnp.dot(mid, tri_ones, preferred_element_type=f32)
                + jnp.dot(lo, tri_ones, preferred_element_type=f32)) * LOG2E

    def score_pass(step, ring, car, res, sc_ref):
        q = q_ref[step // steps_per_seq]
        base = ring * PAGES_PER_STEP
        car = jnp.where(step % steps_per_seq == 0, 0.0, car)
        cars, m_loc = [], []
        for u in range(PAGES_PER_STEP):
            rows = slice(u * N_HEADS, (u + 1) * N_HEADS)
            cum = res[rows, :page]
            part = [None] * N_PARTIAL
            for s in range(page):
                t = kbuf[base + u, s] * q - cum * onehot_ref[s]
                sc = jnp.broadcast_to(jnp.sum(t, axis=-1, keepdims=True), (N_HEADS, LANES))
                sc_ref[u, s] = sc
                i = s % N_PARTIAL
                part[i] = sc if part[i] is None else jnp.maximum(part[i], sc)
                yield sc
            cars.append(car)
            m_loc.append(functools.reduce(jnp.maximum, part) - car)
            car = car + res[rows, page:]
        return tuple(m_loc), tuple(cars), car

    def value_pass(step, ring, soft, scored, sc_ref, tie_ref):
        m_loc, cars, _ = scored
        base = ring * PAGES_PER_STEP
        first = step % steps_per_seq == 0
        m_run = jnp.where(first, NEG_INF, soft[0])
        l_run = jnp.where(first, 0.0, soft[1])
        a_run = jnp.where(first, 0.0, soft[2])
        m_new = functools.reduce(jnp.maximum, list(m_loc) + [m_run])
        alpha = jnp.exp2(m_run - m_new)
        l_parts, a_parts = [l_run * alpha], [a_run * alpha]
        recent = []
        for u in range(PAGES_PER_STEP):
            shift = m_new + cars[u]
            lp = [None] * N_PARTIAL
            ap = [None] * N_PARTIAL
            for s in range(page):
                recent.append((yield))
                logit = sc_ref[u, s]
                if len(recent) > TIE_LAG:
                    logit = jnp.where(never, recent.pop(0), logit)
                pr = jnp.exp2(logit - shift)
                pv = pr * vbuf[base + u, s]
                i = s % N_PARTIAL
                lp[i] = pr if lp[i] is None else lp[i] + pr
                ap[i] = pv if ap[i] is None else ap[i] + pv
            l_parts += lp
            a_parts += ap
        return m_new, _tree_sum(l_parts), _tree_sum(a_parts)

    def write_output(step, soft, car_end):
        b = step // steps_per_seq
        m_run, l_run, a_run = soft

        @pl.when(step % steps_per_seq == last_of_seq)
        def _():
            q = q_ref[b]
            s_new = jnp.sum(kn_ref[b] * q, axis=-1, keepdims=True) - (car_end + lfn_ref[b] * LOG2E)
            m_all = jnp.maximum(m_run, s_new)
            w = jnp.exp2(m_run - m_all)
            w_new = jnp.exp2(s_new - m_all)
            o = (a_run * w + w_new * vn_ref[b]) / (l_run * w + w_new)
            ms = jnp.sum(jnp.sum(o * o, axis=-1, keepdims=True), axis=0, keepdims=True) / D_ATT
            o_ref[b] = o * lax.rsqrt(ms + EPS) * g_ref[...]

    def half(step, ring, soft, scored, res_nxt):
        nxt = step + 1
        sc_cur, sc_nxt = (sc_a, sc_b) if ring % 2 == 0 else (sc_b, sc_a)

        @pl.when(nxt + FETCH_AHEAD < n_steps)
        def _():
            start_fetch(nxt + FETCH_AHEAD)

        @pl.when(nxt < n_steps)
        def _():
            wait_fetch(nxt, (0, 1))

        @pl.when(nxt + 1 < n_steps)
        def _():
            wait_fetch(nxt + 1, (2,))

        res_after = page_forget((ring + 2) % RING_STEPS)
        scored_nxt, soft = _lockstep(
            score_pass(jnp.minimum(nxt, n_steps - 1), (ring + 1) % RING_STEPS, scored[2], res_nxt, sc_nxt),
            value_pass(step, ring, soft, scored, sc_cur, sc_nxt))
        write_output(step, soft, scored[2])
        return soft, scored_nxt, res_after

    assert n_steps > FETCH_AHEAD
    for step in range(FETCH_AHEAD + 1):
        start_fetch(step)
    wait_fetch(0, (0, 1, 2))
    wait_fetch(1, (2,))
    zero = jnp.zeros((N_HEADS, LANES), f32)
    scored0, = _alternate(score_pass(0, 0, zero, page_forget(0), sc_a))

    def trip(t, carry):
        soft, scored, res_nxt = carry
        for ring in range(RING_STEPS):
            soft, scored, res_nxt = half(t * RING_STEPS + ring, ring, soft, scored, res_nxt)
        return soft, scored, res_nxt

    lax.fori_loop(0, n_steps // RING_STEPS, trip, ((zero, zero, zero), scored0, page_forget(1)))


def _decode(page_table, q, k_new, v_new, lf_new, cache_k, cache_v, cache_lf_t, g_att):
    n_seq, n_pages = page_table.shape
    page = cache_k.shape[1]
    assert n_pages % PAGES_PER_STEP == 0 and page == LANES
    assert (n_seq * n_pages // PAGES_PER_STEP) % RING_STEPS == 0
    pt = page_table.reshape(-1)
    tok = pl.BlockSpec((n_seq, N_HEADS, HEAD_DIM), lambda i, pt: (0, 0, 0))
    hbm = pl.BlockSpec(memory_space=pl.ANY)
    grid_spec = pltpu.PrefetchScalarGridSpec(
        num_scalar_prefetch=1,
        grid=(1,),
        in_specs=[tok, tok, tok, tok, pl.BlockSpec((N_HEADS, HEAD_DIM), lambda i, pt: (0, 0)),
                  hbm, hbm, hbm],
        out_specs=tok,
        scratch_shapes=[pltpu.VMEM((PAGE_SLOTS, page, N_HEADS, HEAD_DIM), f32),
                        pltpu.VMEM((PAGE_SLOTS, page, N_HEADS, HEAD_DIM), f32),
                        pltpu.VMEM((PAGE_SLOTS, N_HEADS, page), f32),
                        pltpu.VMEM((PAGES_PER_STEP, page, N_HEADS, LANES), f32),
                        pltpu.VMEM((PAGES_PER_STEP, page, N_HEADS, LANES), f32),
                        pltpu.VMEM((page, 2 * page), bf16),
                        pltpu.VMEM((page, N_HEADS, LANES), f32),
                        pltpu.SemaphoreType.DMA((3, PAGE_SLOTS))],
    )
    return pl.pallas_call(
        functools.partial(_decode_kernel, n_pages=n_pages),
        grid_spec=grid_spec,
        out_shape=jax.ShapeDtypeStruct((n_seq, N_HEADS, HEAD_DIM), f32),
        compiler_params=_params(("arbitrary",), 32),
        name="decode",
    )(pt, q, k_new, v_new, lf_new, g_att.reshape(N_HEADS, HEAD_DIM), cache_k, cache_v, cache_lf_t)


def _out_proj_kernel(att_ref, lru_ref, x_ref, w_ref, g_ref, x1_ref, xn_ref):
    mix = jnp.concatenate([att_ref[...].astype(bf16), lru_ref[...].astype(bf16)], axis=1)
    x1 = x_ref[...] + jnp.dot(mix, w_ref[...], preferred_element_type=f32)
    x1_ref[...] = x1
    xn_ref[...] = _rms(x1, g_ref[...]).astype(bf16)


def _out_proj(att_n, lru_n, x, w_out, g_ffn, *, tm):
    m = x.shape[0]
    row = lambda i: (i, 0)
    const = lambda i: (0, 0)
    return pl.pallas_call(
        _out_proj_kernel,
        grid=(m // tm,),
        in_specs=[pl.BlockSpec((tm, D_ATT), row), pl.BlockSpec((tm, D_LRU), row),
                  pl.BlockSpec((tm, D_MODEL), row),
                  pl.BlockSpec((D_MODEL, D_MODEL), const, pipeline_mode=pl.Buffered(1)),
                  pl.BlockSpec((1, D_MODEL), const)],
        out_specs=(pl.BlockSpec((tm, D_MODEL), row), pl.BlockSpec((tm, D_MODEL), row)),
        out_shape=(jax.ShapeDtypeStruct((m, D_MODEL), f32), jax.ShapeDtypeStruct((m, D_MODEL), bf16)),
        compiler_params=_params(("parallel",), 48),
        name="out_proj",
    )(att_n, lru_n, x, w_out, g_ffn.reshape(1, D_MODEL))


def _ffn_kernel(xn_ref, x1_ref, xns_ref, x1s_ref, wu_ref, wd_ref, g_ref, o_ref, os_ref):
    i = pl.program_id(0)
    f = pl.program_id(1)
    last_i = pl.num_programs(0) - 1
    last_f = pl.num_programs(1) - 1
    tm = xn_ref.shape[0]

    def mlp(x):
        hf = jnp.dot(x, wu_ref[...].astype(bf16), preferred_element_type=f32)
        return jnp.dot(jnp.square(jnp.maximum(hf, 0.0)).astype(bf16), wd_ref[...].astype(bf16),
                       preferred_element_type=f32)

    @pl.when(f == 0)
    def _():
        o_ref[...] = x1_ref[...]

    @pl.when(i < last_i)
    def _():
        o_ref[...] += mlp(xn_ref[...])

    @pl.when(i == last_i)
    def _():
        @pl.when(f == 0)
        def _():
            os_ref[...] = x1s_ref[...]

        y = mlp(jnp.concatenate([xn_ref[...], xns_ref[...]], axis=0))
        o_ref[...] += y[:tm]
        os_ref[...] += y[tm:]

        @pl.when(f == last_f)
        def _():
            os_ref[...] = _rms(os_ref[...], g_ref[...])

    @pl.when(f == last_f)
    def _():
        o_ref[...] = _rms(o_ref[...], g_ref[...])


def _ffn(xn, x1, xn_s, x1_s, w_up, w_down, g_final, *, tm, tf):
    m, ms = x1.shape[0], x1_s.shape[0]
    row = lambda i, f: (i, 0)
    const = lambda i, f: (0, 0)
    return pl.pallas_call(
        _ffn_kernel,
        grid=(m // tm, D_FF // tf),
        in_specs=[pl.BlockSpec((tm, D_MODEL), row, pipeline_mode=pl.Buffered(1)),
                  pl.BlockSpec((tm, D_MODEL), row, pipeline_mode=pl.Buffered(1)),
                  pl.BlockSpec((ms, D_MODEL), const), pl.BlockSpec((ms, D_MODEL), const),
                  pl.BlockSpec((D_MODEL, tf), lambda i, f: (0, f)),
                  pl.BlockSpec((tf, D_MODEL), lambda i, f: (f, 0)),
                  pl.BlockSpec((1, D_MODEL), const)],
        out_specs=(pl.BlockSpec((tm, D_MODEL), row), pl.BlockSpec((ms, D_MODEL), const)),
        out_shape=(jax.ShapeDtypeStruct((m, D_MODEL), f32), jax.ShapeDtypeStruct((ms, D_MODEL), f32)),
        compiler_params=_params(("arbitrary", "arbitrary"), 56),
        name="ffn",
    )(xn, x1, xn_s, x1_s, w_up, w_down, g_final.reshape(1, D_MODEL))


def kernel(x_prompt, x_sample, cache_k, cache_v, cache_logf, state_conv, state_h, page_table,
           g_mix, w_in, b_f, w_conv, b_conv, w_gate_a, b_gate_a, w_gate_x, b_gate_x,
           lru_lambda, g_att_out, g_lru_out, w_out, g_ffn, w_up, w_down, g_final):
    depth = w_in.shape[0]
    n_seq, seq, _ = x_prompt.shape
    n_dec = x_sample.shape[0]
    assert depth == 1 and x_sample.shape[1] == 1

    l = 0
    c_f = 3 * D_ATT
    w_t = jnp.pad(w_in[l].T, ((0, c_f + W_TAIL_COLS - w_in.shape[2]), (0, 0))).astype(bf16)
    wa = w_gate_a[l].astype(bf16)
    wx = w_gate_x[l].astype(bf16)
    wo = w_out[l].astype(bf16)
    wu = w_up[l]
    wd = w_down[l]
    lru_w = (w_conv[l], b_conv[l], wa, wx, b_gate_a[l], b_gate_x[l], lru_lambda[l], g_lru_out[l])

    xp = x_prompt.reshape(n_seq * seq, D_MODEL)
    xs = x_sample.reshape(n_dec, D_MODEL)
    (q, k, v, kb, vb, xl, gl, lft), (qs, ks, vs, xls, gls, lfts) = _in_proj(
        xp, xs, g_mix[l], w_t, b_f[l], tm=256)
    lru_n, h_t = _lru(xl, gl, *lru_w, n_seq=n_seq, seq=seq, tb=256)
    att_n = _attn(q, kb, vb, lft, g_att_out[l], n_seq=n_seq, seq=seq, blk=256, kblk=256)
    x1, xn = _out_proj(att_n, lru_n, xp, wo, g_ffn[l], tm=512)

    new_k_p = k.reshape(1, n_seq, seq, N_HEADS, HEAD_DIM)
    new_v_p = v.reshape(1, n_seq, seq, N_HEADS, HEAD_DIM)
    new_f_p = lft[:N_HEADS].T.reshape(1, n_seq, seq, N_HEADS)
    new_c_p = xl.reshape(n_seq, seq, D_LRU)[:, seq - (CONV_W - 1):, :][None]
    new_h_p = h_t.reshape(1, n_seq, D_LRU)

    lf_s = lfts[:N_HEADS].T
    heads = lambda t: t.reshape(n_dec, N_HEADS, HEAD_DIM)
    lf_wide = jnp.broadcast_to(lf_s[:, :, None], (n_dec, N_HEADS, HEAD_DIM))
    att_s = _decode(page_table, heads(qs), heads(ks), heads(vs), lf_wide, cache_k[l], cache_v[l],
                    jnp.swapaxes(cache_logf[l], 1, 2), g_att_out[l]).reshape(n_dec, D_ATT)
    sc = state_conv[l]
    lru_s, h_s = _lru_step(xls, gls, sc[:, 0], sc[:, 1], sc[:, 2], state_h[l], *lru_w)
    x1s, xns = _out_proj(att_s, lru_s, xs, wo, g_ffn[l], tm=n_dec)

    y_prompt, y_sample = _ffn(xn, x1, xns, x1s, wu, wd, g_final, tm=1024, tf=512)
    y_prompt = y_prompt.reshape(n_seq, seq, D_MODEL)
    y_sample = y_sample.reshape(n_dec, 1, D_MODEL)

    new_k_s = ks.reshape(1, n_dec, 1, N_HEADS, HEAD_DIM)
    new_v_s = vs.reshape(1, n_dec, 1, N_HEADS, HEAD_DIM)
    new_f_s = lf_s.reshape(1, n_dec, 1, N_HEADS)
    new_c_s = jnp.stack([sc[:, 1], sc[:, 2], xls], axis=1)[None]
    new_h_s = h_s[None]

    return (y_prompt, y_sample, new_k_p, new_v_p, new_f_p, new_c_p, new_h_p,
            new_k_s, new_v_s, new_f_s, new_c_s, new_h_s)
```

```python
import functools

import jax
import jax.numpy as jnp
from jax import lax
from jax.experimental import pallas as pl
from jax.experimental.pallas import tpu as pltpu

D_MODEL = 2048
D_ATT = 1024
D_LRU = 1024
HEAD_DIM = 128
N_HEADS = 8
N_LRU_BLOCKS = 8
LRU_BLOCK = 128
CONV_W = 4
LRU_C = 8.0
D_FF = 4 * D_MODEL
EPS = 1e-6
NEG_INF = -1e30
LOG2E = 1.4426950408889634
F_ROWS = 16
W_TAIL_COLS = 17 * 128
SUBLANES = 8
LANES = 128
MIB = 1024 * 1024

f32 = jnp.float32
bf16 = jnp.bfloat16

IN_PROJ_ROWS = 256
LRU_CHUNK = 512
ATTN_BLOCK = 256
OUT_PROJ_ROWS = 512
FFN_ROWS = 1024
FFN_COLS = 512
VMEM_MIB = {"in_proj": 52, "lru": 40, "attn": 40, "decode": 32, "out_proj": 48, "ffn": 58}


def _rms(x, g):
    return x * lax.rsqrt(jnp.mean(x * x, axis=-1, keepdims=True) + EPS) * g


def _softplus(x):
    return jnp.maximum(x, 0.0) + jnp.log1p(jnp.exp(-jnp.abs(x)))


def _sigmoid(x):
    return 0.5 * jnp.tanh(0.5 * x) + 0.5


def _gelu_tanh(x):
    c = 0.7978845608028654
    return x * (0.5 * (1.0 + jnp.tanh(c * (x + 0.044715 * (x * x * x)))))


def _split3(x):
    hi = x.astype(bf16)
    r1 = x - hi.astype(f32)
    mid = r1.astype(bf16)
    lo = (r1 - mid.astype(f32)).astype(bf16)
    return hi, mid, lo


def _params(sem, vmem_mib, flags=None):
    return pltpu.CompilerParams(dimension_semantics=sem, vmem_limit_bytes=vmem_mib * MIB, flags=flags)


def _in_proj_kernel(x_ref, xs_ref, g_ref, wt_ref, bf_ref,
                    q_ref, k_ref, v_ref, kb_ref, vb_ref, xl_ref, gl_ref, lft_ref,
                    qs_ref, ks_ref, vs_ref, xls_ref, gls_ref, lfts_ref):
    i = pl.program_id(0)
    last = pl.num_programs(0) - 1
    tm = x_ref.shape[0]

    def project(x):
        xn = _rms(x, g_ref[...]).astype(bf16)

        def cols(c0, width):
            return lax.dot_general(xn, wt_ref[c0:c0 + width, :], (((1,), (1,)), ((), ())),
                                   preferred_element_type=f32)

        c_f = 3 * D_ATT
        t = lax.dot_general(wt_ref[c_f:c_f + F_ROWS, :], xn, (((1,), (1,)), ((), ())),
                            preferred_element_type=f32) + bf_ref[:, :x.shape[0]]
        lf = jnp.minimum(t, 0.0) - jnp.log1p(jnp.exp(-jnp.abs(t)))
        tail = cols(c_f, W_TAIL_COLS)
        xl = tail[:, N_HEADS:N_HEADS + D_LRU]
        gl = tail[:, N_HEADS + D_LRU:N_HEADS + 2 * D_LRU]
        return (cols(0, D_ATT) * (HEAD_DIM ** -0.5 * LOG2E), cols(D_ATT, D_ATT), cols(2 * D_ATT, D_ATT),
                xl, gl, lf)

    def store_prompt(q, k, v, xl, gl, lf):
        q_ref[...] = q.astype(q_ref.dtype)
        k_ref[...] = k
        kb_ref[...] = k.astype(bf16)
        v_ref[...] = v
        vb_ref[...] = v.astype(bf16)
        xl_ref[...] = xl
        gl_ref[...] = gl
        lft_ref[...] = lf

    @pl.when(i < last)
    def _():
        store_prompt(*project(x_ref[...]))

    @pl.when(i == last)
    def _():
        q, k, v, xl, gl, lf = project(jnp.concatenate([x_ref[...], xs_ref[...]], axis=0))
        store_prompt(q[:tm], k[:tm], v[:tm], xl[:tm], gl[:tm], lf[:, :tm])
        qs_ref[...] = q[tm:]
        ks_ref[...] = k[tm:]
        vs_ref[...] = v[tm:]
        xls_ref[...] = xl[tm:]
        gls_ref[...] = gl[tm:]
        lfts_ref[...] = lf[:, tm:]


def _in_proj(x, x_s, g_mix, w_t, b_f, *, tm):
    m, ms = x.shape[0], x_s.shape[0]
    bfb = jnp.broadcast_to(jnp.pad(b_f, (0, F_ROWS - N_HEADS))[:, None], (F_ROWS, tm + ms)).astype(f32)
    row = lambda i: (i, 0)
    const = lambda i: (0, 0)
    wide = pl.BlockSpec((tm, D_ATT), row)
    small = pl.BlockSpec((ms, D_ATT), const)
    wide_shape = lambda dt: jax.ShapeDtypeStruct((m, D_ATT), dt)
    small_shape = jax.ShapeDtypeStruct((ms, D_ATT), f32)
    outs = pl.pallas_call(
        _in_proj_kernel,
        grid=(m // tm,),
        in_specs=[
            pl.BlockSpec((tm, D_MODEL), row),
            pl.BlockSpec((ms, D_MODEL), const),
            pl.BlockSpec((1, D_MODEL), const),
            pl.BlockSpec((3 * D_ATT + W_TAIL_COLS, D_MODEL), const, pipeline_mode=pl.Buffered(1)),
            pl.BlockSpec((F_ROWS, tm + ms), const),
        ],
        out_specs=(wide,) * 7 + (pl.BlockSpec((F_ROWS, tm), lambda i: (0, i)),)
        + (small,) * 5 + (pl.BlockSpec((F_ROWS, ms), const),),
        out_shape=(wide_shape(bf16), wide_shape(f32), wide_shape(f32), wide_shape(bf16), wide_shape(bf16),
                   wide_shape(f32), wide_shape(f32), jax.ShapeDtypeStruct((F_ROWS, m), f32))
        + (small_shape,) * 5 + (jax.ShapeDtypeStruct((F_ROWS, ms), f32),),
        compiler_params=_params(("arbitrary",), VMEM_MIB["in_proj"]),
        name="in_proj",
    )(x, x_s, g_mix.reshape(1, D_MODEL), w_t, bfb)
    return outs[:8], outs[8:]


def _lru_gates(xc, wa_ref, wx_ref, ba, bx, lam):
    xcb = xc.astype(bf16)
    r_parts, i_parts = [], []
    for n in range(N_LRU_BLOCKS):
        blk = xcb[:, n * LRU_BLOCK:(n + 1) * LRU_BLOCK]
        r_parts.append(jnp.dot(blk, wa_ref[n], preferred_element_type=f32))
        i_parts.append(jnp.dot(blk, wx_ref[n], preferred_element_type=f32))
    r = _sigmoid(jnp.concatenate(r_parts, axis=1) + ba)
    i = _sigmoid(jnp.concatenate(i_parts, axis=1) + bx)
    log_a = (-LRU_C * r) * _softplus(-lam)
    a = jnp.exp(log_a)
    th = jnp.tanh(log_a)
    u = -2.0 * th
    b = jnp.where(u > 0.0, u * lax.rsqrt(u * (1.0 - th)), 0.0) * i * xc
    return a, b


def _lru_kernel(xl_ref, gl_ref, wc_ref, bc_ref, wa_ref, wx_ref, ba_ref, bx_ref, lam_ref, g_ref,
                out_ref, ht_ref, ext_ref, h_ref, hs_ref):
    c = pl.program_id(1)
    tb = xl_ref.shape[0]

    @pl.when(c == 0)
    def _():
        ext_ref[0:SUBLANES, :] = jnp.zeros((SUBLANES, D_LRU), f32)
        h_ref[...] = jnp.zeros((SUBLANES, D_LRU), f32)

    @pl.when(c > 0)
    def _():
        ext_ref[0:SUBLANES, :] = ext_ref[tb:tb + SUBLANES, :]

    row = lax.broadcasted_iota(jnp.int32, (tb, D_LRU), 0) & (SUBLANES - 1)

    def shift_in_group(x, s):
        return pltpu.roll(x.reshape(tb // SUBLANES, SUBLANES, D_LRU), s, 1).reshape(tb, D_LRU)

    x = xl_ref[...]
    ext_ref[SUBLANES:tb + SUBLANES, :] = x
    x_up = ext_ref[0:tb, :]

    def delayed(s):
        return jnp.where(row >= s, shift_in_group(x, s), shift_in_group(x_up, s))

    wc = wc_ref[...]
    xc = bc_ref[...] + wc[0:1] * delayed(3)
    xc = xc + wc[1:2] * delayed(2)
    xc = xc + wc[2:3] * delayed(1)
    xc = xc + wc[3:4] * x

    a, b = _lru_gates(xc, wa_ref, wx_ref, ba_ref[...], bx_ref[...], lam_ref[...])

    for s in (1, 2, 4):
        a_sh = shift_in_group(a, s)
        b_sh = shift_in_group(b, s)
        take = row >= s
        b = jnp.where(take, a * b_sh + b, b)
        a = jnp.where(take, a * a_sh, a)

    carry = h_ref[...]
    for g in range(tb // SUBLANES):
        sl = slice(g * SUBLANES, (g + 1) * SUBLANES)
        hg = a[sl] * carry + b[sl]
        hs_ref[sl, :] = hg
        carry = jnp.broadcast_to(hg[SUBLANES - 1:SUBLANES, :], (SUBLANES, D_LRU))
    h_ref[...] = carry

    lru = _gelu_tanh(gl_ref[...]) * hs_ref[...]
    out_ref[...] = _rms(lru, g_ref[...]).astype(out_ref.dtype)

    @pl.when(c == pl.num_programs(1) - 1)
    def _():
        ht_ref[0] = carry[0:1]


def _lru(xl, gl, w_conv, b_conv, wa, wx, ba, bx, lam, g_lru, *, n_seq, seq, tb):
    nc = seq // tb
    row = lambda b, c: (b * nc + c, 0)
    const2 = lambda b, c: (0, 0)
    const3 = lambda b, c: (0, 0, 0)
    vec = pl.BlockSpec((1, D_LRU), const2)
    return pl.pallas_call(
        _lru_kernel,
        grid=(n_seq, nc),
        in_specs=[
            pl.BlockSpec((tb, D_LRU), row), pl.BlockSpec((tb, D_LRU), row),
            pl.BlockSpec((CONV_W, D_LRU), const2), vec,
            pl.BlockSpec((N_LRU_BLOCKS, LRU_BLOCK, LRU_BLOCK), const3),
            pl.BlockSpec((N_LRU_BLOCKS, LRU_BLOCK, LRU_BLOCK), const3),
            vec, vec, vec, vec,
        ],
        out_specs=(pl.BlockSpec((tb, D_LRU), row),
                   pl.BlockSpec((1, 1, D_LRU), lambda b, c: (b, 0, 0))),
        out_shape=(jax.ShapeDtypeStruct((n_seq * seq, D_LRU), bf16),
                   jax.ShapeDtypeStruct((n_seq, 1, D_LRU), f32)),
        scratch_shapes=[pltpu.VMEM((tb + SUBLANES, D_LRU), f32),
                        pltpu.VMEM((SUBLANES, D_LRU), f32),
                        pltpu.VMEM((tb, D_LRU), f32)],
        compiler_params=_params(("parallel", "arbitrary"), VMEM_MIB["lru"]),
        name="lru",
    )(xl, gl, w_conv, b_conv.reshape(1, D_LRU), wa, wx, ba.reshape(1, D_LRU),
      bx.reshape(1, D_LRU), lam.reshape(1, D_LRU), g_lru.reshape(1, D_LRU))


def _lru_step_kernel(xl_ref, gl_ref, s0_ref, s1_ref, s2_ref, h0_ref, wc_ref, bc_ref, wa_ref, wx_ref,
                     ba_ref, bx_ref, lam_ref, g_ref, out_ref, h_ref):
    wc = wc_ref[...]
    xc = bc_ref[...] + wc[0:1] * s0_ref[...]
    xc = xc + wc[1:2] * s1_ref[...]
    xc = xc + wc[2:3] * s2_ref[...]
    xc = xc + wc[3:4] * xl_ref[...]
    a, b = _lru_gates(xc, wa_ref, wx_ref, ba_ref[...], bx_ref[...], lam_ref[...])
    h = a * h0_ref[...] + b
    h_ref[...] = h
    out_ref[...] = _rms(_gelu_tanh(gl_ref[...]) * h, g_ref[...]).astype(out_ref.dtype)


def _lru_step(xl, gl, s0, s1, s2, h0, w_conv, b_conv, wa, wx, ba, bx, lam, g_lru):
    n = xl.shape[0]
    return pl.pallas_call(
        _lru_step_kernel,
        out_shape=(jax.ShapeDtypeStruct((n, D_LRU), bf16), jax.ShapeDtypeStruct((n, D_LRU), f32)),
        name="lru_step",
    )(xl, gl, s0, s1, s2, h0, w_conv, b_conv.reshape(1, D_LRU), wa, wx, ba.reshape(1, D_LRU),
      bx.reshape(1, D_LRU), lam.reshape(1, D_LRU), g_lru.reshape(1, D_LRU))


def _attn_kernel(q_ref, k_ref, v_ref, lft_ref, g_ref, o_ref, nf_ref, att_ref, m_ref, l_ref, *, blk, kblk):
    qi = pl.program_id(1)
    n_kblk = nf_ref.shape[0]
    per_q = blk // kblk

    @pl.when(qi == 0)
    def _():
        r = lax.broadcasted_iota(jnp.int32, (kblk, 2 * kblk), 0)
        c = lax.broadcasted_iota(jnp.int32, (kblk, 2 * kblk), 1)
        tri_ones = jnp.where((r <= c) | (c >= kblk), 1.0, 0.0).astype(bf16)
        carry = jnp.zeros((F_ROWS, kblk), f32)
        for j in range(n_kblk):
            hi, mid, lo = _split3(lft_ref[:, j * kblk:(j + 1) * kblk])
            res = (jnp.dot(hi, tri_ones, preferred_element_type=f32)
                   + jnp.dot(mid, tri_ones, preferred_element_type=f32)
                   + jnp.dot(lo, tri_ones, preferred_element_type=f32))
            cum = carry + res[:, :kblk]
            nf_ref[j] = -LOG2E * cum[0:N_HEADS]
            carry = carry + res[:, kblk:]

    rows = lax.broadcasted_iota(jnp.int32, (blk, kblk), 0)
    colsi = lax.broadcasted_iota(jnp.int32, (blk, kblk), 1)

    m_ref[...] = jnp.full(m_ref.shape, NEG_INF, f32)
    l_ref[...] = jnp.zeros(l_ref.shape, f32)
    att_ref[...] = jnp.zeros(att_ref.shape, f32)

    def kv_block(j, diag):
        off = pl.multiple_of(j * kblk, kblk)
        nf = nf_ref[j]
        for h in range(N_HEADS):
            hs = slice(h * HEAD_DIM, (h + 1) * HEAD_DIM)
            s = lax.dot_general(q_ref[:, hs], k_ref[pl.ds(off, kblk), hs], (((1,), (1,)), ((), ())),
                                preferred_element_type=f32) + nf[h:h + 1, :]
            if diag is not None:
                s = jnp.where(colsi + diag * kblk <= rows, s, NEG_INF)
            m_old = m_ref[h]
            m_new = jnp.maximum(m_old, jnp.max(s, axis=-1, keepdims=True))
            alpha = jnp.exp2(m_old - m_new)
            p = jnp.exp2(s - jnp.concatenate([m_new] * (kblk // LANES), axis=1))
            m_ref[h] = m_new
            l_ref[h] = alpha * l_ref[h] + jnp.sum(p, axis=-1, keepdims=True)
            att_ref[:, hs] = alpha * att_ref[:, hs] + jnp.dot(
                p.astype(bf16), v_ref[pl.ds(off, kblk), hs], preferred_element_type=f32)

    def body(j, carry):
        kv_block(j, None)
        return carry

    lax.fori_loop(0, qi * per_q, body, 0)
    for d in range(per_q):
        kv_block(qi * per_q + d, d)
    for h in range(N_HEADS):
        hs = slice(h * HEAD_DIM, (h + 1) * HEAD_DIM)
        att_ref[:, hs] = att_ref[:, hs] / l_ref[h]
    o_ref[...] = _rms(att_ref[...], g_ref[...]).astype(o_ref.dtype)


def _attn(q, kb, vb, lft, g_att, *, n_seq, seq, blk, kblk):
    nq = seq // blk
    return pl.pallas_call(
        functools.partial(_attn_kernel, blk=blk, kblk=kblk),
        grid=(n_seq, nq),
        in_specs=[
            pl.BlockSpec((blk, D_ATT), lambda b, i: (b * nq + i, 0)),
            pl.BlockSpec((seq, D_ATT), lambda b, i: (b, 0)),
            pl.BlockSpec((seq, D_ATT), lambda b, i: (b, 0)),
            pl.BlockSpec((F_ROWS, seq), lambda b, i: (0, b)),
            pl.BlockSpec((1, D_ATT), lambda b, i: (0, 0)),
        ],
        out_specs=pl.BlockSpec((blk, D_ATT), lambda b, i: (b * nq + i, 0)),
        out_shape=jax.ShapeDtypeStruct((n_seq * seq, D_ATT), bf16),
        scratch_shapes=[pltpu.VMEM((seq // kblk, N_HEADS, kblk), f32), pltpu.VMEM((blk, D_ATT), f32),
                        pltpu.VMEM((N_HEADS, blk, LANES), f32), pltpu.VMEM((N_HEADS, blk, LANES), f32)],
        compiler_params=_params(("parallel", "arbitrary"), VMEM_MIB["attn"]),
        name="attn",
    )(q, kb, vb, lft, g_att.reshape(1, D_ATT))


PAGES_PER_STEP = 2
RING_STEPS = 8
FETCH_AHEAD = RING_STEPS - 2
PAGE_SLOTS = RING_STEPS * PAGES_PER_STEP
N_PARTIAL = 4
TIE_LAG = 24


def _tree_sum(xs):
    while len(xs) > 1:
        xs = [a + b for a, b in zip(xs[0::2], xs[1::2])] + xs[len(xs) & ~1:]
    return xs[0]


def _lockstep(lead, follow):
    next(follow)
    lead_result = follow_result = None
    while True:
        try:
            fresh = next(lead)
        except StopIteration as done:
            lead_result = done.value
            break
        try:
            follow.send(fresh)
        except StopIteration as done:
            follow_result = done.value
    return lead_result, follow_result


def _alternate(*passes):
    results = [None] * len(passes)
    live = list(range(len(passes)))
    while live:
        for i in list(live):
            try:
                next(passes[i])
            except StopIteration as done:
                results[i] = done.value
                live.remove(i)
    return results


def _decode_kernel(pt_ref, q_ref, kn_ref, vn_ref, lfn_ref, g_ref, k_hbm, v_hbm, lf_hbm,
                   o_ref, kbuf, vbuf, lfbuf, sc_a, sc_b, tri_ref, onehot_ref, sem, *, n_pages):
    page = kbuf.shape[1]
    steps_per_seq = n_pages // PAGES_PER_STEP
    n_steps = o_ref.shape[0] * steps_per_seq
    last_of_seq = steps_per_seq - 1

    def page_copies(slot, pid):
        return (pltpu.make_async_copy(k_hbm.at[pid], kbuf.at[slot], sem.at[0, slot]),
                pltpu.make_async_copy(v_hbm.at[pid], vbuf.at[slot], sem.at[1, slot]),
                pltpu.make_async_copy(lf_hbm.at[pid], lfbuf.at[slot], sem.at[2, slot]))

    def slot_base(step):
        return (step % RING_STEPS) * PAGES_PER_STEP

    def start_fetch(step):
        for u in range(PAGES_PER_STEP):
            for cp in page_copies(slot_base(step) + u, pt_ref[step * PAGES_PER_STEP + u]):
                cp.start()

    def wait_fetch(step, streams):
        for u in range(PAGES_PER_STEP):
            copies = page_copies(slot_base(step) + u, 0)
            for stream in streams:
                copies[stream].wait()

    r = lax.broadcasted_iota(jnp.int32, (page, 2 * page), 0)
    c = lax.broadcasted_iota(jnp.int32, (page, 2 * page), 1)
    tri_ref[...] = jnp.where((r <= c) | (c >= page), 1.0, 0.0).astype(bf16)
    lane = lax.broadcasted_iota(jnp.int32, (N_HEADS, LANES), 1)
    for s in range(page):
        onehot_ref[s] = jnp.where(lane == s, 1.0, 0.0)
    never = pt_ref[0] < 0

    def page_forget(ring):
        base = ring * PAGES_PER_STEP
        lf2 = jnp.concatenate([lfbuf[base + u] for u in range(PAGES_PER_STEP)], axis=0)
        hi, mid, lo = _split3(lf2)
        tri_ones = tri_ref[...]
        return (jnp.dot(hi, tri_ones, preferred_element_type=f32)
                + jnp.dot(mid, tri_ones, preferred_element_type=f32)
                + jnp.dot(lo, tri_ones, preferred_element_type=f32)) * LOG2E

    def score_pass(step, ring, car, res, sc_ref):
        q = q_ref[step // steps_per_seq]
        base = ring * PAGES_PER_STEP
        car = jnp.where(step % steps_per_seq == 0, 0.0, car)
        cars, m_loc = [], []
        for u in range(PAGES_PER_STEP):
            rows = slice(u * N_HEADS, (u + 1) * N_HEADS)
            cum = res[rows, :page]
            part = [None] * N_PARTIAL
            for s in range(page):
                t = kbuf[base + u, s] * q - cum * onehot_ref[s]
                sc = jnp.broadcast_to(jnp.sum(t, axis=-1, keepdims=True), (N_HEADS, LANES))
                sc_ref[u, s] = sc
                i = s % N_PARTIAL
                part[i] = sc if part[i] is None else jnp.maximum(part[i], sc)
                yield sc
            cars.append(car)
            m_loc.append(functools.reduce(jnp.maximum, part) - car)
            car = car + res[rows, page:]
        return tuple(m_loc), tuple(cars), car

    def value_pass(step, ring, soft, scored, sc_ref, tie_ref):
        m_loc, cars, _ = scored
        base = ring * PAGES_PER_STEP
        first = step % steps_per_seq == 0
        m_run = jnp.where(first, NEG_INF, soft[0])
        l_run = jnp.where(first, 0.0, soft[1])
        a_run = jnp.where(first, 0.0, soft[2])
        m_new = functools.reduce(jnp.maximum, list(m_loc) + [m_run])
        alpha = jnp.exp2(m_run - m_new)
        l_parts, a_parts = [l_run * alpha], [a_run * alpha]
        recent = []
        for u in range(PAGES_PER_STEP):
            shift = m_new + cars[u]
            lp = [None] * N_PARTIAL
            ap = [None] * N_PARTIAL
            for s in range(page):
                recent.append((yield))
                logit = sc_ref[u, s]
                if len(recent) > TIE_LAG:
                    logit = jnp.where(never, recent.pop(0), logit)
                pr = jnp.exp2(logit - shift)
                pv = pr * vbuf[base + u, s]
                i = s % N_PARTIAL
                lp[i] = pr if lp[i] is None else lp[i] + pr
                ap[i] = pv if ap[i] is None else ap[i] + pv
            l_parts += lp
            a_parts += ap
        return m_new, _tree_sum(l_parts), _tree_sum(a_parts)

    def write_output(step, soft, car_end):
        b = step // steps_per_seq
        m_run, l_run, a_run = soft

        @pl.when(step % steps_per_seq == last_of_seq)
        def _():
            q = q_ref[b]
            s_new = jnp.sum(kn_ref[b] * q, axis=-1, keepdims=True) - (car_end + lfn_ref[b] * LOG2E)
            m_all = jnp.maximum(m_run, s_new)
            w = jnp.exp2(m_run - m_all)
            w_new = jnp.exp2(s_new - m_all)
            o = (a_run * w + w_new * vn_ref[b]) / (l_run * w + w_new)
            ms = jnp.sum(jnp.sum(o * o, axis=-1, keepdims=True), axis=0, keepdims=True) / D_ATT
            o_ref[b] = o * lax.rsqrt(ms + EPS) * g_ref[...]

    def half(step, ring, soft, scored, res_nxt):
        nxt = step + 1
        sc_cur, sc_nxt = (sc_a, sc_b) if ring % 2 == 0 else (sc_b, sc_a)

        @pl.when(nxt + FETCH_AHEAD < n_steps)
        def _():
            start_fetch(nxt + FETCH_AHEAD)

        @pl.when(nxt < n_steps)
        def _():
            wait_fetch(nxt, (0, 1))

        @pl.when(nxt + 1 < n_steps)
        def _():
            wait_fetch(nxt + 1, (2,))

        res_after = page_forget((ring + 2) % RING_STEPS)
        scored_nxt, soft = _lockstep(
            score_pass(jnp.minimum(nxt, n_steps - 1), (ring + 1) % RING_STEPS, scored[2], res_nxt, sc_nxt),
            value_pass(step, ring, soft, scored, sc_cur, sc_nxt))
        write_output(step, soft, scored[2])
        return soft, scored_nxt, res_after

    assert n_steps > FETCH_AHEAD
    for step in range(FETCH_AHEAD + 1):
        start_fetch(step)
    wait_fetch(0, (0, 1, 2))
    wait_fetch(1, (2,))
    zero = jnp.zeros((N_HEADS, LANES), f32)
    scored0, = _alternate(score_pass(0, 0, zero, page_forget(0), sc_a))

    def trip(t, carry):
        soft, scored, res_nxt = carry
        for ring in range(RING_STEPS):
            soft, scored, res_nxt = half(t * RING_STEPS + ring, ring, soft, scored, res_nxt)
        return soft, scored, res_nxt

    lax.fori_loop(0, n_steps // RING_STEPS, trip, ((zero, zero, zero), scored0, page_forget(1)))


def _decode(page_table, q, k_new, v_new, lf_new, cache_k, cache_v, cache_lf_t, g_att):
    n_seq, n_pages = page_table.shape
    page = cache_k.shape[1]
    assert n_pages % PAGES_PER_STEP == 0 and page == LANES
    assert (n_seq * n_pages // PAGES_PER_STEP) % RING_STEPS == 0
    pt = page_table.reshape(-1)
    tok = pl.BlockSpec((n_seq, N_HEADS, HEAD_DIM), lambda i, pt: (0, 0, 0))
    hbm = pl.BlockSpec(memory_space=pl.ANY)
    grid_spec = pltpu.PrefetchScalarGridSpec(
        num_scalar_prefetch=1,
        grid=(1,),
        in_specs=[tok, tok, tok, tok, pl.BlockSpec((N_HEADS, HEAD_DIM), lambda i, pt: (0, 0)),
                  hbm, hbm, hbm],
        out_specs=tok,
        scratch_shapes=[pltpu.VMEM((PAGE_SLOTS, page, N_HEADS, HEAD_DIM), f32),
                        pltpu.VMEM((PAGE_SLOTS, page, N_HEADS, HEAD_DIM), f32),
                        pltpu.VMEM((PAGE_SLOTS, N_HEADS, page), f32),
                        pltpu.VMEM((PAGES_PER_STEP, page, N_HEADS, LANES), f32),
                        pltpu.VMEM((PAGES_PER_STEP, page, N_HEADS, LANES), f32),
                        pltpu.VMEM((page, 2 * page), bf16),
                        pltpu.VMEM((page, N_HEADS, LANES), f32),
                        pltpu.SemaphoreType.DMA((3, PAGE_SLOTS))],
    )
    return pl.pallas_call(
        functools.partial(_decode_kernel, n_pages=n_pages),
        grid_spec=grid_spec,
        out_shape=jax.ShapeDtypeStruct((n_seq, N_HEADS, HEAD_DIM), f32),
        compiler_params=_params(("arbitrary",), VMEM_MIB["decode"]),
        name="decode",
    )(pt, q, k_new, v_new, lf_new, g_att.reshape(N_HEADS, HEAD_DIM), cache_k, cache_v, cache_lf_t)


def _out_proj_kernel(att_ref, lru_ref, x_ref, w_ref, g_ref, x1_ref, xn_ref):
    mix = jnp.concatenate([att_ref[...].astype(bf16), lru_ref[...].astype(bf16)], axis=1)
    x1 = x_ref[...] + jnp.dot(mix, w_ref[...], preferred_element_type=f32)
    x1_ref[...] = x1
    xn_ref[...] = _rms(x1, g_ref[...]).astype(bf16)


def _out_proj(att_n, lru_n, x, w_out, g_ffn, *, tm):
    m = x.shape[0]
    row = lambda i: (i, 0)
    const = lambda i: (0, 0)
    return pl.pallas_call(
        _out_proj_kernel,
        grid=(m // tm,),
        in_specs=[pl.BlockSpec((tm, D_ATT), row), pl.BlockSpec((tm, D_LRU), row),
                  pl.BlockSpec((tm, D_MODEL), row),
                  pl.BlockSpec((D_MODEL, D_MODEL), const, pipeline_mode=pl.Buffered(1)),
                  pl.BlockSpec((1, D_MODEL), const)],
        out_specs=(pl.BlockSpec((tm, D_MODEL), row), pl.BlockSpec((tm, D_MODEL), row)),
        out_shape=(jax.ShapeDtypeStruct((m, D_MODEL), f32), jax.ShapeDtypeStruct((m, D_MODEL), bf16)),
        compiler_params=_params(("parallel",), VMEM_MIB["out_proj"]),
        name="out_proj",
    )(att_n, lru_n, x, w_out, g_ffn.reshape(1, D_MODEL))


def _ffn_kernel(xn_ref, x1_ref, xns_ref, x1s_ref, wu_ref, wd_ref, g_ref, o_ref, os_ref):
    i = pl.program_id(0)
    f = pl.program_id(1)
    last_i = pl.num_programs(0) - 1
    last_f = pl.num_programs(1) - 1
    tm = xn_ref.shape[0]

    def mlp(x):
        hf = jnp.dot(x, wu_ref[...].astype(bf16), preferred_element_type=f32)
        return jnp.dot(jnp.square(jnp.maximum(hf, 0.0)).astype(bf16), wd_ref[...].astype(bf16),
                       preferred_element_type=f32)

    @pl.when(f == 0)
    def _():
        o_ref[...] = x1_ref[...]

    @pl.when(i < last_i)
    def _():
        o_ref[...] += mlp(xn_ref[...])

    @pl.when(i == last_i)
    def _():
        @pl.when(f == 0)
        def _():
            os_ref[...] = x1s_ref[...]

        y = mlp(jnp.concatenate([xn_ref[...], xns_ref[...]], axis=0))
        o_ref[...] += y[:tm]
        os_ref[...] += y[tm:]

        @pl.when(f == last_f)
        def _():
            os_ref[...] = _rms(os_ref[...], g_ref[...])

    @pl.when(f == last_f)
    def _():
        o_ref[...] = _rms(o_ref[...], g_ref[...])


def _ffn(xn, x1, xn_s, x1_s, w_up, w_down, g_final, *, tm, tf):
    m, ms = x1.shape[0], x1_s.shape[0]
    row = lambda i, f: (i, 0)
    const = lambda i, f: (0, 0)
    return pl.pallas_call(
        _ffn_kernel,
        grid=(m // tm, D_FF // tf),
        in_specs=[pl.BlockSpec((tm, D_MODEL), row),
                  pl.BlockSpec((tm, D_MODEL), row, pipeline_mode=pl.Buffered(1)),
                  pl.BlockSpec((ms, D_MODEL), const), pl.BlockSpec((ms, D_MODEL), const),
                  pl.BlockSpec((D_MODEL, tf), lambda i, f: (0, f)),
                  pl.BlockSpec((tf, D_MODEL), lambda i, f: (f, 0)),
                  pl.BlockSpec((1, D_MODEL), const)],
        out_specs=(pl.BlockSpec((tm, D_MODEL), row), pl.BlockSpec((ms, D_MODEL), const)),
        out_shape=(jax.ShapeDtypeStruct((m, D_MODEL), f32), jax.ShapeDtypeStruct((ms, D_MODEL), f32)),
        compiler_params=_params(("arbitrary", "arbitrary"), VMEM_MIB["ffn"]),
        name="ffn",
    )(xn, x1, xn_s, x1_s, w_up, w_down, g_final.reshape(1, D_MODEL))


def kernel(x_prompt, x_sample, cache_k, cache_v, cache_logf, state_conv, state_h, page_table,
           g_mix, w_in, b_f, w_conv, b_conv, w_gate_a, b_gate_a, w_gate_x, b_gate_x,
           lru_lambda, g_att_out, g_lru_out, w_out, g_ffn, w_up, w_down, g_final):
    depth = w_in.shape[0]
    n_seq, seq, _ = x_prompt.shape
    n_dec = x_sample.shape[0]
    assert depth == 1 and x_sample.shape[1] == 1

    l = 0
    c_f = 3 * D_ATT
    w_t = jnp.pad(w_in[l].T, ((0, c_f + W_TAIL_COLS - w_in.shape[2]), (0, 0))).astype(bf16)
    wa = w_gate_a[l].astype(bf16)
    wx = w_gate_x[l].astype(bf16)
    wo = w_out[l].astype(bf16)
    wu = w_up[l]
    wd = w_down[l]
    lru_w = (w_conv[l], b_conv[l], wa, wx, b_gate_a[l], b_gate_x[l], lru_lambda[l], g_lru_out[l])

    xp = x_prompt.reshape(n_seq * seq, D_MODEL)
    xs = x_sample.reshape(n_dec, D_MODEL)
    (q, k, v, kb, vb, xl, gl, lft), (qs, ks, vs, xls, gls, lfts) = _in_proj(
        xp, xs, g_mix[l], w_t, b_f[l], tm=IN_PROJ_ROWS)
    lru_n, h_t = _lru(xl, gl, *lru_w, n_seq=n_seq, seq=seq, tb=LRU_CHUNK)
    att_n = _attn(q, kb, vb, lft, g_att_out[l], n_seq=n_seq, seq=seq, blk=ATTN_BLOCK, kblk=ATTN_BLOCK)
    x1, xn = _out_proj(att_n, lru_n, xp, wo, g_ffn[l], tm=OUT_PROJ_ROWS)

    new_k_p = k.reshape(1, n_seq, seq, N_HEADS, HEAD_DIM)
    new_v_p = v.reshape(1, n_seq, seq, N_HEADS, HEAD_DIM)
    new_f_p = lft[:N_HEADS].T.reshape(1, n_seq, seq, N_HEADS)
    new_c_p = xl.reshape(n_seq, seq, D_LRU)[:, seq - (CONV_W - 1):, :][None]
    new_h_p = h_t.reshape(1, n_seq, D_LRU)

    lf_s = lfts[:N_HEADS].T
    heads = lambda t: t.reshape(n_dec, N_HEADS, HEAD_DIM)
    lf_wide = jnp.broadcast_to(lf_s[:, :, None], (n_dec, N_HEADS, HEAD_DIM))
    att_s = _decode(page_table, heads(qs), heads(ks), heads(vs), lf_wide, cache_k[l], cache_v[l],
                    jnp.swapaxes(cache_logf[l], 1, 2), g_att_out[l]).reshape(n_dec, D_ATT)
    sc = state_conv[l]
    lru_s, h_s = _lru_step(xls, gls, sc[:, 0], sc[:, 1], sc[:, 2], state_h[l], *lru_w)
    x1s, xns = _out_proj(att_s, lru_s, xs, wo, g_ffn[l], tm=n_dec)

    y_prompt, y_sample = _ffn(xn, x1, xns, x1s, wu, wd, g_final, tm=FFN_ROWS, tf=FFN_COLS)
    y_prompt = y_prompt.reshape(n_seq, seq, D_MODEL)
    y_sample = y_sample.reshape(n_dec, 1, D_MODEL)

    new_k_s = ks.reshape(1, n_dec, 1, N_HEADS, HEAD_DIM)
    new_v_s = vs.reshape(1, n_dec, 1, N_HEADS, HEAD_DIM)
    new_f_s = lf_s.reshape(1, n_dec, 1, N_HEADS)
    new_c_s = jnp.stack([sc[:, 1], sc[:, 2], xls], axis=1)[None]
    new_h_s = h_s[None]

    return (y_prompt, y_sample, new_k_p, new_v_p, new_f_p, new_c_p, new_h_p,
            new_k_s, new_v_s, new_f_s, new_c_s, new_h_s)
```

```python
import functools

import jax
import jax.numpy as jnp
from jax import lax
from jax.experimental import pallas as pl
from jax.experimental.pallas import tpu as pltpu

D_MODEL = 2048
D_ATT = 1024
D_LRU = 1024
HEAD_DIM = 128
N_HEADS = 8
N_LRU_BLOCKS = 8
LRU_BLOCK = 128
CONV_W = 4
LRU_C = 8.0
D_FF = 4 * D_MODEL
EPS = 1e-6
NEG_INF = -1e30
LOG2E = 1.4426950408889634
F_ROWS = 16
W_TAIL_COLS = 17 * 128
SUBLANES = 8
LANES = 128
MIB = 1024 * 1024

f32 = jnp.float32
bf16 = jnp.bfloat16

IN_PROJ_ROWS = 256
LRU_CHUNK = 512
ATTN_BLOCK = 256
OUT_PROJ_ROWS = 512
FFN_ROWS = 1024
FFN_COLS = 512
VMEM_MIB = {"in_proj": 52, "lru": 40, "attn": 40, "decode": 32, "out_proj": 48, "ffn": 58}


def _rms(x, g):
    return x * lax.rsqrt(jnp.mean(x * x, axis=-1, keepdims=True) + EPS) * g


def _softplus(x):
    return jnp.maximum(x, 0.0) + jnp.log1p(jnp.exp(-jnp.abs(x)))


def _sigmoid(x):
    return 0.5 * jnp.tanh(0.5 * x) + 0.5


def _gelu_tanh(x):
    c = 0.7978845608028654
    return x * (0.5 * (1.0 + jnp.tanh(c * (x + 0.044715 * (x * x * x)))))


def _split3(x):
    hi = x.astype(bf16)
    r1 = x - hi.astype(f32)
    mid = r1.astype(bf16)
    lo = (r1 - mid.astype(f32)).astype(bf16)
    return hi, mid, lo


def _params(sem, vmem_mib, flags=None):
    return pltpu.CompilerParams(dimension_semantics=sem, vmem_limit_bytes=vmem_mib * MIB, flags=flags)


def _in_proj_kernel(x_ref, xs_ref, g_ref, wt_ref, bf_ref,
                    q_ref, k_ref, v_ref, kb_ref, vb_ref, xl_ref, gl_ref, lft_ref,
                    qs_ref, ks_ref, vs_ref, xls_ref, gls_ref, lfts_ref):
    i = pl.program_id(0)
    last = pl.num_programs(0) - 1
    tm = x_ref.shape[0]

    def project(x):
        xn = _rms(x, g_ref[...]).astype(bf16)

        def cols(c0, width):
            return lax.dot_general(xn, wt_ref[c0:c0 + width, :], (((1,), (1,)), ((), ())),
                                   preferred_element_type=f32)

        c_f = 3 * D_ATT
        t = lax.dot_general(wt_ref[c_f:c_f + F_ROWS, :], xn, (((1,), (1,)), ((), ())),
                            preferred_element_type=f32) + bf_ref[:, :x.shape[0]]
        lf = jnp.minimum(t, 0.0) - jnp.log1p(jnp.exp(-jnp.abs(t)))
        tail = cols(c_f, W_TAIL_COLS)
        xl = tail[:, N_HEADS:N_HEADS + D_LRU]
        gl = tail[:, N_HEADS + D_LRU:N_HEADS + 2 * D_LRU]
        return (cols(0, D_ATT) * (HEAD_DIM ** -0.5 * LOG2E), cols(D_ATT, D_ATT), cols(2 * D_ATT, D_ATT),
                xl, gl, lf)

    def store_prompt(q, k, v, xl, gl, lf):
        q_ref[...] = q.astype(q_ref.dtype)
        k_ref[...] = k
        kb_ref[...] = k.astype(bf16)
        v_ref[...] = v
        vb_ref[...] = v.astype(bf16)
        xl_ref[...] = xl
        gl_ref[...] = gl
        lft_ref[...] = lf

    @pl.when(i < last)
    def _():
        store_prompt(*project(x_ref[...]))

    @pl.when(i == last)
    def _():
        q, k, v, xl, gl, lf = project(jnp.concatenate([x_ref[...], xs_ref[...]], axis=0))
        store_prompt(q[:tm], k[:tm], v[:tm], xl[:tm], gl[:tm], lf[:, :tm])
        qs_ref[...] = q[tm:]
        ks_ref[...] = k[tm:]
        vs_ref[...] = v[tm:]
        xls_ref[...] = xl[tm:]
        gls_ref[...] = gl[tm:]
        lfts_ref[...] = lf[:, tm:]


def _in_proj(x, x_s, g_mix, w_t, b_f, *, tm):
    m, ms = x.shape[0], x_s.shape[0]
    bfb = jnp.broadcast_to(jnp.pad(b_f, (0, F_ROWS - N_HEADS))[:, None], (F_ROWS, tm + ms)).astype(f32)
    row = lambda i: (i, 0)
    const = lambda i: (0, 0)
    wide = pl.BlockSpec((tm, D_ATT), row)
    small = pl.BlockSpec((ms, D_ATT), const)
    wide_shape = lambda dt: jax.ShapeDtypeStruct((m, D_ATT), dt)
    small_shape = jax.ShapeDtypeStruct((ms, D_ATT), f32)
    outs = pl.pallas_call(
        _in_proj_kernel,
        grid=(m // tm,),
        in_specs=[
            pl.BlockSpec((tm, D_MODEL), row),
            pl.BlockSpec((ms, D_MODEL), const),
            pl.BlockSpec((1, D_MODEL), const),
            pl.BlockSpec((3 * D_ATT + W_TAIL_COLS, D_MODEL), const, pipeline_mode=pl.Buffered(1)),
            pl.BlockSpec((F_ROWS, tm + ms), const),
        ],
        out_specs=(wide,) * 7 + (pl.BlockSpec((F_ROWS, tm), lambda i: (0, i)),)
        + (small,) * 5 + (pl.BlockSpec((F_ROWS, ms), const),),
        out_shape=(wide_shape(bf16), wide_shape(f32), wide_shape(f32), wide_shape(bf16), wide_shape(bf16),
                   wide_shape(f32), wide_shape(f32), jax.ShapeDtypeStruct((F_ROWS, m), f32))
        + (small_shape,) * 5 + (jax.ShapeDtypeStruct((F_ROWS, ms), f32),),
        compiler_params=_params(("arbitrary",), VMEM_MIB["in_proj"]),
        name="in_proj",
    )(x, x_s, g_mix.reshape(1, D_MODEL), w_t, bfb)
    return outs[:8], outs[8:]


def _lru_gates(xc, wa_ref, wx_ref, ba, bx, lam):
    xcb = xc.astype(bf16)
    r_parts, i_parts = [], []
    for n in range(N_LRU_BLOCKS):
        blk = xcb[:, n * LRU_BLOCK:(n + 1) * LRU_BLOCK]
        r_parts.append(jnp.dot(blk, wa_ref[n], preferred_element_type=f32))
        i_parts.append(jnp.dot(blk, wx_ref[n], preferred_element_type=f32))
    r = _sigmoid(jnp.concatenate(r_parts, axis=1) + ba)
    i = _sigmoid(jnp.concatenate(i_parts, axis=1) + bx)
    log_a = (-LRU_C * r) * _softplus(-lam)
    a = jnp.exp(log_a)
    th = jnp.tanh(log_a)
    u = -2.0 * th
    b = jnp.where(u > 0.0, u * lax.rsqrt(u * (1.0 - th)), 0.0) * i * xc
    return a, b


def _lru_kernel(xl_ref, gl_ref, wc_ref, bc_ref, wa_ref, wx_ref, ba_ref, bx_ref, lam_ref, g_ref,
                out_ref, ht_ref, ext_ref, h_ref, hs_ref):
    c = pl.program_id(1)
    tb = xl_ref.shape[0]

    @pl.when(c == 0)
    def _():
        ext_ref[0:SUBLANES, :] = jnp.zeros((SUBLANES, D_LRU), f32)
        h_ref[...] = jnp.zeros((SUBLANES, D_LRU), f32)

    @pl.when(c > 0)
    def _():
        ext_ref[0:SUBLANES, :] = ext_ref[tb:tb + SUBLANES, :]

    row = lax.broadcasted_iota(jnp.int32, (tb, D_LRU), 0) & (SUBLANES - 1)

    def shift_in_group(x, s):
        return pltpu.roll(x.reshape(tb // SUBLANES, SUBLANES, D_LRU), s, 1).reshape(tb, D_LRU)

    x = xl_ref[...]
    ext_ref[SUBLANES:tb + SUBLANES, :] = x
    x_up = ext_ref[0:tb, :]

    def delayed(s):
        return jnp.where(row >= s, shift_in_group(x, s), shift_in_group(x_up, s))

    wc = wc_ref[...]
    xc = bc_ref[...] + wc[0:1] * delayed(3)
    xc = xc + wc[1:2] * delayed(2)
    xc = xc + wc[2:3] * delayed(1)
    xc = xc + wc[3:4] * x

    a, b = _lru_gates(xc, wa_ref, wx_ref, ba_ref[...], bx_ref[...], lam_ref[...])

    for s in (1, 2, 4):
        a_sh = shift_in_group(a, s)
        b_sh = shift_in_group(b, s)
        take = row >= s
        b = jnp.where(take, a * b_sh + b, b)
        a = jnp.where(take, a * a_sh, a)

    carry = h_ref[...]
    for g in range(tb // SUBLANES):
        sl = slice(g * SUBLANES, (g + 1) * SUBLANES)
        hg = a[sl] * carry + b[sl]
        hs_ref[sl, :] = hg
        carry = jnp.broadcast_to(hg[SUBLANES - 1:SUBLANES, :], (SUBLANES, D_LRU))
    h_ref[...] = carry

    lru = _gelu_tanh(gl_ref[...]) * hs_ref[...]
    out_ref[...] = _rms(lru, g_ref[...]).astype(out_ref.dtype)

    @pl.when(c == pl.num_programs(1) - 1)
    def _():
        ht_ref[0] = carry[0:1]


def _lru(xl, gl, w_conv, b_conv, wa, wx, ba, bx, lam, g_lru, *, n_seq, seq, tb):
    nc = seq // tb
    row = lambda b, c: (b * nc + c, 0)
    const2 = lambda b, c: (0, 0)
    const3 = lambda b, c: (0, 0, 0)
    vec = pl.BlockSpec((1, D_LRU), const2)
    return pl.pallas_call(
        _lru_kernel,
        grid=(n_seq, nc),
        in_specs=[
            pl.BlockSpec((tb, D_LRU), row), pl.BlockSpec((tb, D_LRU), row),
            pl.BlockSpec((CONV_W, D_LRU), const2), vec,
            pl.BlockSpec((N_LRU_BLOCKS, LRU_BLOCK, LRU_BLOCK), const3),
            pl.BlockSpec((N_LRU_BLOCKS, LRU_BLOCK, LRU_BLOCK), const3),
            vec, vec, vec, vec,
        ],
        out_specs=(pl.BlockSpec((tb, D_LRU), row),
                   pl.BlockSpec((1, 1, D_LRU), lambda b, c: (b, 0, 0))),
        out_shape=(jax.ShapeDtypeStruct((n_seq * seq, D_LRU), bf16),
                   jax.ShapeDtypeStruct((n_seq, 1, D_LRU), f32)),
        scratch_shapes=[pltpu.VMEM((tb + SUBLANES, D_LRU), f32),
                        pltpu.VMEM((SUBLANES, D_LRU), f32),
                        pltpu.VMEM((tb, D_LRU), f32)],
        compiler_params=_params(("parallel", "arbitrary"), VMEM_MIB["lru"]),
        name="lru",
    )(xl, gl, w_conv, b_conv.reshape(1, D_LRU), wa, wx, ba.reshape(1, D_LRU),
      bx.reshape(1, D_LRU), lam.reshape(1, D_LRU), g_lru.reshape(1, D_LRU))


def _lru_step_kernel(xl_ref, gl_ref, s0_ref, s1_ref, s2_ref, h0_ref, wc_ref, bc_ref, wa_ref, wx_ref,
                     ba_ref, bx_ref, lam_ref, g_ref, out_ref, h_ref):
    wc = wc_ref[...]
    xc = bc_ref[...] + wc[0:1] * s0_ref[...]
    xc = xc + wc[1:2] * s1_ref[...]
    xc = xc + wc[2:3] * s2_ref[...]
    xc = xc + wc[3:4] * xl_ref[...]
    a, b = _lru_gates(xc, wa_ref, wx_ref, ba_ref[...], bx_ref[...], lam_ref[...])
    h = a * h0_ref[...] + b
    h_ref[...] = h
    out_ref[...] = _rms(_gelu_tanh(gl_ref[...]) * h, g_ref[...]).astype(out_ref.dtype)


def _lru_step(xl, gl, s0, s1, s2, h0, w_conv, b_conv, wa, wx, ba, bx, lam, g_lru):
    n = xl.shape[0]
    return pl.pallas_call(
        _lru_step_kernel,
        out_shape=(jax.ShapeDtypeStruct((n, D_LRU), bf16), jax.ShapeDtypeStruct((n, D_LRU), f32)),
        name="lru_step",
    )(xl, gl, s0, s1, s2, h0, w_conv, b_conv.reshape(1, D_LRU), wa, wx, ba.reshape(1, D_LRU),
      bx.reshape(1, D_LRU), lam.reshape(1, D_LRU), g_lru.reshape(1, D_LRU))


def _attn_kernel(q_ref, k_ref, v_ref, lft_ref, g_ref, o_ref, nf_ref, att_ref, m_ref, l_ref, *, blk, kblk):
    qi = pl.program_id(1)
    n_kblk = nf_ref.shape[0]
    per_q = blk // kblk

    @pl.when(qi == 0)
    def _():
        r = lax.broadcasted_iota(jnp.int32, (kblk, 2 * kblk), 0)
        c = lax.broadcasted_iota(jnp.int32, (kblk, 2 * kblk), 1)
        tri_ones = jnp.where((r <= c) | (c >= kblk), 1.0, 0.0).astype(bf16)
        carry = jnp.zeros((F_ROWS, kblk), f32)
        for j in range(n_kblk):
            hi, mid, lo = _split3(lft_ref[:, j * kblk:(j + 1) * kblk])
            res = (jnp.dot(hi, tri_ones, preferred_element_type=f32)
                   + jnp.dot(mid, tri_ones, preferred_element_type=f32)
                   + jnp.dot(lo, tri_ones, preferred_element_type=f32))
            cum = carry + res[:, :kblk]
            nf_ref[j] = -LOG2E * cum[0:N_HEADS]
            carry = carry + res[:, kblk:]

    rows = lax.broadcasted_iota(jnp.int32, (blk, kblk), 0)
    colsi = lax.broadcasted_iota(jnp.int32, (blk, kblk), 1)

    m_ref[...] = jnp.full(m_ref.shape, NEG_INF, f32)
    l_ref[...] = jnp.zeros(l_ref.shape, f32)
    att_ref[...] = jnp.zeros(att_ref.shape, f32)

    def kv_block(j, diag):
        off = pl.multiple_of(j * kblk, kblk)
        nf = nf_ref[j]
        for h in range(N_HEADS):
            hs = slice(h * HEAD_DIM, (h + 1) * HEAD_DIM)
            s = lax.dot_general(q_ref[:, hs], k_ref[pl.ds(off, kblk), hs], (((1,), (1,)), ((), ())),
                                preferred_element_type=f32) + nf[h:h + 1, :]
            if diag is not None:
                s = jnp.where(colsi + diag * kblk <= rows, s, NEG_INF)
            m_old = m_ref[h]
            m_new = jnp.maximum(m_old, jnp.max(s, axis=-1, keepdims=True))
            alpha = jnp.exp2(m_old - m_new)
            p = jnp.exp2(s - jnp.concatenate([m_new] * (kblk // LANES), axis=1))
            m_ref[h] = m_new
            l_ref[h] = alpha * l_ref[h] + jnp.sum(p, axis=-1, keepdims=True)
            att_ref[:, hs] = alpha * att_ref[:, hs] + jnp.dot(
                p.astype(bf16), v_ref[pl.ds(off, kblk), hs], preferred_element_type=f32)

    def body(j, carry):
        kv_block(j, None)
        return carry

    lax.fori_loop(0, qi * per_q, body, 0)
    for d in range(per_q):
        kv_block(qi * per_q + d, d)
    for h in range(N_HEADS):
        hs = slice(h * HEAD_DIM, (h + 1) * HEAD_DIM)
        att_ref[:, hs] = att_ref[:, hs] / l_ref[h]
    o_ref[...] = _rms(att_ref[...], g_ref[...]).astype(o_ref.dtype)


def _attn(q, kb, vb, lft, g_att, *, n_seq, seq, blk, kblk):
    nq = seq // blk
    return pl.pallas_call(
        functools.partial(_attn_kernel, blk=blk, kblk=kblk),
        grid=(n_seq, nq),
        in_specs=[
            pl.BlockSpec((blk, D_ATT), lambda b, i: (b * nq + i, 0)),
            pl.BlockSpec((seq, D_ATT), lambda b, i: (b, 0)),
            pl.BlockSpec((seq, D_ATT), lambda b, i: (b, 0)),
            pl.BlockSpec((F_ROWS, seq), lambda b, i: (0, b)),
            pl.BlockSpec((1, D_ATT), lambda b, i: (0, 0)),
        ],
        out_specs=pl.BlockSpec((blk, D_ATT), lambda b, i: (b * nq + i, 0)),
        out_shape=jax.ShapeDtypeStruct((n_seq * seq, D_ATT), bf16),
        scratch_shapes=[pltpu.VMEM((seq // kblk, N_HEADS, kblk), f32), pltpu.VMEM((blk, D_ATT), f32),
                        pltpu.VMEM((N_HEADS, blk, LANES), f32), pltpu.VMEM((N_HEADS, blk, LANES), f32)],
        compiler_params=_params(("parallel", "arbitrary"), VMEM_MIB["attn"]),
        name="attn",
    )(q, kb, vb, lft, g_att.reshape(1, D_ATT))


PAGES_PER_STEP = 2
RING_STEPS = 8
FETCH_AHEAD = RING_STEPS - 2
PAGE_SLOTS = RING_STEPS * PAGES_PER_STEP
N_PARTIAL = 4
TIE_LAG = 24


def _tree_sum(xs):
    while len(xs) > 1:
        xs = [a + b for a, b in zip(xs[0::2], xs[1::2])] + xs[len(xs) & ~1:]
    return xs[0]


def _lockstep(lead, follow):
    next(follow)
    lead_result = follow_result = None
    while True:
        try:
            fresh = next(lead)
        except StopIteration as done:
            lead_result = done.value
            break
        try:
            follow.send(fresh)
        except StopIteration as done:
            follow_result = done.value
    return lead_result, follow_result


def _alternate(*passes):
    results = [None] * len(passes)
    live = list(range(len(passes)))
    while live:
        for i in list(live):
            try:
                next(passes[i])
            except StopIteration as done:
                results[i] = done.value
                live.remove(i)
    return results


def _decode_kernel(pt_ref, q_ref, kn_ref, vn_ref, lfn_ref, g_ref, k_hbm, v_hbm, lf_hbm,
                   o_ref, kbuf, vbuf, lfbuf, sc_a, sc_b, tri_ref, onehot_ref, sem, *, n_pages):
    page = kbuf.shape[1]
    steps_per_seq = n_pages // PAGES_PER_STEP
    n_steps = o_ref.shape[0] * steps_per_seq
    last_of_seq = steps_per_seq - 1

    def page_copies(slot, pid):
        return (pltpu.make_async_copy(k_hbm.at[pid], kbuf.at[slot], sem.at[0, slot]),
                pltpu.make_async_copy(v_hbm.at[pid], vbuf.at[slot], sem.at[1, slot]),
                pltpu.make_async_copy(lf_hbm.at[pid], lfbuf.at[slot], sem.at[2, slot]))

    def slot_base(step):
        return (step % RING_STEPS) * PAGES_PER_STEP

    def start_fetch(step):
        for u in range(PAGES_PER_STEP):
            for cp in page_copies(slot_base(step) + u, pt_ref[step * PAGES_PER_STEP + u]):
                cp.start()

    def wait_fetch(step, streams):
        for u in range(PAGES_PER_STEP):
            copies = page_copies(slot_base(step) + u, 0)
            for stream in streams:
                copies[stream].wait()

    r = lax.broadcasted_iota(jnp.int32, (page, 2 * page), 0)
    c = lax.broadcasted_iota(jnp.int32, (page, 2 * page), 1)
    tri_ref[...] = jnp.where((r <= c) | (c >= page), 1.0, 0.0).astype(bf16)
    lane = lax.broadcasted_iota(jnp.int32, (N_HEADS, LANES), 1)
    for s in range(page):
        onehot_ref[s] = jnp.where(lane == s, 1.0, 0.0)
    never = pt_ref[0] < 0

    def page_forget(ring):
        base = ring * PAGES_PER_STEP
        lf2 = jnp.concatenate([lfbuf[base + u] for u in range(PAGES_PER_STEP)], axis=0)
        hi, mid, lo = _split3(lf2)
        tri_ones = tri_ref[...]
        return (jnp.dot(hi, tri_ones, preferred_element_type=f32)
                + jnp.dot(mid, tri_ones, preferred_element_type=f32)
                + jnp.dot(lo, tri_ones, preferred_element_type=f32)) * LOG2E

    def score_pass(step, ring, car, res, sc_ref):
        q = q_ref[step // steps_per_seq]
        base = ring * PAGES_PER_STEP
        car = jnp.where(step % steps_per_seq == 0, 0.0, car)
        cars, m_loc = [], []
        for u in range(PAGES_PER_STEP):
            rows = slice(u * N_HEADS, (u + 1) * N_HEADS)
            cum = res[rows, :page]
            part = [None] * N_PARTIAL
            for s in range(page):
                t = kbuf[base + u, s] * q - cum * onehot_ref[s]
                sc = jnp.broadcast_to(jnp.sum(t, axis=-1, keepdims=True), (N_HEADS, LANES))
                sc_ref[u, s] = sc
                i = s % N_PARTIAL
                part[i] = sc if part[i] is None else jnp.maximum(part[i], sc)
                yield sc
            cars.append(car)
            m_loc.append(functools.reduce(jnp.maximum, part) - car)
            car = car + res[rows, page:]
        return tuple(m_loc), tuple(cars), car

    def value_pass(step, ring, soft, scored, sc_ref, tie_ref):
        m_loc, cars, _ = scored
        base = ring * PAGES_PER_STEP
        first = step % steps_per_seq == 0
        m_run = jnp.where(first, NEG_INF, soft[0])
        l_run = jnp.where(first, 0.0, soft[1])
        a_run = jnp.where(first, 0.0, soft[2])
        m_new = functools.reduce(jnp.maximum, list(m_loc) + [m_run])
        alpha = jnp.exp2(m_run - m_new)
        l_parts, a_parts = [l_run * alpha], [a_run * alpha]
        recent = []
        for u in range(PAGES_PER_STEP):
            shift = m_new + cars[u]
            lp = [None] * N_PARTIAL
            ap = [None] * N_PARTIAL
            for s in range(page):
                recent.append((yield))
                logit = sc_ref[u, s]
                if len(recent) > TIE_LAG:
                    logit = jnp.where(never, recent.pop(0), logit)
                pr = jnp.exp2(logit - shift)
                pv = pr * vbuf[base + u, s]
                i = s % N_PARTIAL
                lp[i] = pr if lp[i] is None else lp[i] + pr
                ap[i] = pv if ap[i] is None else ap[i] + pv
            l_parts += lp
            a_parts += ap
        return m_new, _tree_sum(l_parts), _tree_sum(a_parts)

    def write_output(step, soft, car_end):
        b = step // steps_per_seq
        m_run, l_run, a_run = soft

        @pl.when(step % steps_per_seq == last_of_seq)
        def _():
            q = q_ref[b]
            s_new = jnp.sum(kn_ref[b] * q, axis=-1, keepdims=True) - (car_end + lfn_ref[b] * LOG2E)
            m_all = jnp.maximum(m_run, s_new)
            w = jnp.exp2(m_run - m_all)
            w_new = jnp.exp2(s_new - m_all)
            o = (a_run * w + w_new * vn_ref[b]) / (l_run * w + w_new)
            ms = jnp.sum(jnp.sum(o * o, axis=-1, keepdims=True), axis=0, keepdims=True) / D_ATT
            o_ref[b] = o * lax.rsqrt(ms + EPS) * g_ref[...]

    def half(step, ring, soft, scored, res_nxt):
        nxt = step + 1
        sc_cur, sc_nxt = (sc_a, sc_b) if ring % 2 == 0 else (sc_b, sc_a)

        @pl.when(nxt + FETCH_AHEAD < n_steps)
        def _():
            start_fetch(nxt + FETCH_AHEAD)

        @pl.when(nxt < n_steps)
        def _():
            wait_fetch(nxt, (0, 1))

        @pl.when(nxt + 1 < n_steps)
        def _():
            wait_fetch(nxt + 1, (2,))

        res_after = page_forget((ring + 2) % RING_STEPS)
        scored_nxt, soft = _lockstep(
            score_pass(jnp.minimum(nxt, n_steps - 1), (ring + 1) % RING_STEPS, scored[2], res_nxt, sc_nxt),
            value_pass(step, ring, soft, scored, sc_cur, sc_nxt))
        write_output(step, soft, scored[2])
        return soft, scored_nxt, res_after

    assert n_steps > FETCH_AHEAD
    for step in range(FETCH_AHEAD + 1):
        start_fetch(step)
    wait_fetch(0, (0, 1, 2))
    wait_fetch(1, (2,))
    zero = jnp.zeros((N_HEADS, LANES), f32)
    scored0, = _alternate(score_pass(0, 0, zero, page_forget(0), sc_a))

    def trip(t, carry):
        soft, scored, res_nxt = carry
        for ring in range(RING_STEPS):
            soft, scored, res_nxt = half(t * RING_STEPS + ring, ring, soft, scored, res_nxt)
        return soft, scored, res_nxt

    lax.fori_loop(0, n_steps // RING_STEPS, trip, ((zero, zero, zero), scored0, page_forget(1)))


def _decode(page_table, q, k_new, v_new, lf_new, cache_k, cache_v, cache_lf_t, g_att):
    n_seq, n_pages = page_table.shape
    page = cache_k.shape[1]
    assert n_pages % PAGES_PER_STEP == 0 and page == LANES
    assert (n_seq * n_pages // PAGES_PER_STEP) % RING_STEPS == 0
    pt = page_table.reshape(-1)
    tok = pl.BlockSpec((n_seq, N_HEADS, HEAD_DIM), lambda i, pt: (0, 0, 0))
    hbm = pl.BlockSpec(memory_space=pl.ANY)
    grid_spec = pltpu.PrefetchScalarGridSpec(
        num_scalar_prefetch=1,
        grid=(1,),
        in_specs=[tok, tok, tok, tok, pl.BlockSpec((N_HEADS, HEAD_DIM), lambda i, pt: (0, 0)),
                  hbm, hbm, hbm],
        out_specs=tok,
        scratch_shapes=[pltpu.VMEM((PAGE_SLOTS, page, N_HEADS, HEAD_DIM), f32),
                        pltpu.VMEM((PAGE_SLOTS, page, N_HEADS, HEAD_DIM), f32),
                        pltpu.VMEM((PAGE_SLOTS, N_HEADS, page), f32),
                        pltpu.VMEM((PAGES_PER_STEP, page, N_HEADS, LANES), f32),
                        pltpu.VMEM((PAGES_PER_STEP, page, N_HEADS, LANES), f32),
                        pltpu.VMEM((page, 2 * page), bf16),
                        pltpu.VMEM((page, N_HEADS, LANES), f32),
                        pltpu.SemaphoreType.DMA((3, PAGE_SLOTS))],
    )
    return pl.pallas_call(
        functools.partial(_decode_kernel, n_pages=n_pages),
        grid_spec=grid_spec,
        out_shape=jax.ShapeDtypeStruct((n_seq, N_HEADS, HEAD_DIM), f32),
        compiler_params=_params(("arbitrary",), VMEM_MIB["decode"]),
        name="decode",
    )(pt, q, k_new, v_new, lf_new, g_att.reshape(N_HEADS, HEAD_DIM), cache_k, cache_v, cache_lf_t)


def _out_proj_kernel(att_ref, lru_ref, x_ref, w_ref, g_ref, x1_ref, xn_ref):
    mix = jnp.concatenate([att_ref[...].astype(bf16), lru_ref[...].astype(bf16)], axis=1)
    x1 = x_ref[...] + jnp.dot(mix, w_ref[...], preferred_element_type=f32)
    x1_ref[...] = x1
    xn_ref[...] = _rms(x1, g_ref[...]).astype(bf16)


def _out_proj(att_n, lru_n, x, w_out, g_ffn, *, tm):
    m = x.shape[0]
    row = lambda i: (i, 0)
    const = lambda i: (0, 0)
    return pl.pallas_call(
        _out_proj_kernel,
        grid=(m // tm,),
        in_specs=[pl.BlockSpec((tm, D_ATT), row), pl.BlockSpec((tm, D_LRU), row),
                  pl.BlockSpec((tm, D_MODEL), row),
                  pl.BlockSpec((D_MODEL, D_MODEL), const, pipeline_mode=pl.Buffered(1)),
                  pl.BlockSpec((1, D_MODEL), const)],
        out_specs=(pl.BlockSpec((tm, D_MODEL), row), pl.BlockSpec((tm, D_MODEL), row)),
        out_shape=(jax.ShapeDtypeStruct((m, D_MODEL), f32), jax.ShapeDtypeStruct((m, D_MODEL), bf16)),
        compiler_params=_params(("parallel",), VMEM_MIB["out_proj"]),
        name="out_proj",
    )(att_n, lru_n, x, w_out, g_ffn.reshape(1, D_MODEL))


def _ffn_kernel(xn_ref, x1_ref, xns_ref, x1s_ref, wu_ref, wd_ref, g_ref, o_ref, os_ref):
    i = pl.program_id(0)
    f = pl.program_id(1)
    last_i = pl.num_programs(0) - 1
    last_f = pl.num_programs(1) - 1
    tm = xn_ref.shape[0]

    def mlp(x):
        hf = jnp.dot(x, wu_ref[...].astype(bf16), preferred_element_type=f32)
        return jnp.dot(jnp.square(jnp.maximum(hf, 0.0)).astype(bf16), wd_ref[...].astype(bf16),
                       preferred_element_type=f32)

    @pl.when(f == 0)
    def _():
        o_ref[...] = x1_ref[...]

    @pl.when(i < last_i)
    def _():
        o_ref[...] += mlp(xn_ref[...])

    @pl.when(i == last_i)
    def _():
        @pl.when(f == 0)
        def _():
            os_ref[...] = x1s_ref[...]

        y = mlp(jnp.concatenate([xn_ref[...], xns_ref[...]], axis=0))
        o_ref[...] += y[:tm]
        os_ref[...] += y[tm:]

        @pl.when(f == last_f)
        def _():
            os_ref[...] = _rms(os_ref[...], g_ref[...])

    @pl.when(f == last_f)
    def _():
        o_ref[...] = _rms(o_ref[...], g_ref[...])


def _ffn(xn, x1, xn_s, x1_s, w_up, w_down, g_final, *, tm, tf):
    m, ms = x1.shape[0], x1_s.shape[0]
    row = lambda i, f: (i, 0)
    const = lambda i, f: (0, 0)
    return pl.pallas_call(
        _ffn_kernel,
        grid=(m // tm, D_FF // tf),
        in_specs=[pl.BlockSpec((tm, D_MODEL), row),
                  pl.BlockSpec((tm, D_MODEL), row, pipeline_mode=pl.Buffered(1)),
                  pl.BlockSpec((ms, D_MODEL), const), pl.BlockSpec((ms, D_MODEL), const),
                  pl.BlockSpec((D_MODEL, tf), lambda i, f: (0, f)),
                  pl.BlockSpec((tf, D_MODEL), lambda i, f: (f, 0)),
                  pl.BlockSpec((1, D_MODEL), const)],
        out_specs=(pl.BlockSpec((tm, D_MODEL), row), pl.BlockSpec((ms, D_MODEL), const)),
        out_shape=(jax.ShapeDtypeStruct((m, D_MODEL), f32), jax.ShapeDtypeStruct((ms, D_MODEL), f32)),
        compiler_params=_params(("arbitrary", "arbitrary"), VMEM_MIB["ffn"]),
        name="ffn",
    )(xn, x1, xn_s, x1_s, w_up, w_down, g_final.reshape(1, D_MODEL))


def kernel(x_prompt, x_sample, cache_k, cache_v, cache_logf, state_conv, state_h, page_table,
           g_mix, w_in, b_f, w_conv, b_conv, w_gate_a, b_gate_a, w_gate_x, b_gate_x,
           lru_lambda, g_att_out, g_lru_out, w_out, g_ffn, w_up, w_down, g_final):
    depth = w_in.shape[0]
    n_seq, seq, _ = x_prompt.shape
    n_dec = x_sample.shape[0]
    assert depth == 1 and x_sample.shape[1] == 1

    l = 0
    c_f = 3 * D_ATT
    w_t = w_in[l].T.astype(bf16)
    wa = w_gate_a[l].astype(bf16)
    wx = w_gate_x[l].astype(bf16)
    wo = w_out[l].astype(bf16)
    wu = w_up[l]
    wd = w_down[l]
    lru_w = (w_conv[l], b_conv[l], wa, wx, b_gate_a[l], b_gate_x[l], lru_lambda[l], g_lru_out[l])

    xp = x_prompt.reshape(n_seq * seq, D_MODEL)
    xs = x_sample.reshape(n_dec, D_MODEL)
    (q, k, v, kb, vb, xl, gl, lft), (qs, ks, vs, xls, gls, lfts) = _in_proj(
        xp, xs, g_mix[l], w_t, b_f[l], tm=IN_PROJ_ROWS)
    lru_n, h_t = _lru(xl, gl, *lru_w, n_seq=n_seq, seq=seq, tb=LRU_CHUNK)
    att_n = _attn(q, kb, vb, lft, g_att_out[l], n_seq=n_seq, seq=seq, blk=ATTN_BLOCK, kblk=ATTN_BLOCK)
    x1, xn = _out_proj(att_n, lru_n, xp, wo, g_ffn[l], tm=OUT_PROJ_ROWS)

    new_k_p = k.reshape(1, n_seq, seq, N_HEADS, HEAD_DIM)
    new_v_p = v.reshape(1, n_seq, seq, N_HEADS, HEAD_DIM)
    new_f_p = lft[:N_HEADS].T.reshape(1, n_seq, seq, N_HEADS)
    new_c_p = xl.reshape(n_seq, seq, D_LRU)[:, seq - (CONV_W - 1):, :][None]
    new_h_p = h_t.reshape(1, n_seq, D_LRU)

    lf_s = lfts[:N_HEADS].T
    heads = lambda t: t.reshape(n_dec, N_HEADS, HEAD_DIM)
    lf_wide = jnp.broadcast_to(lf_s[:, :, None], (n_dec, N_HEADS, HEAD_DIM))
    att_s = _decode(page_table, heads(qs), heads(ks), heads(vs), lf_wide, cache_k[l], cache_v[l],
                    jnp.swapaxes(cache_logf[l], 1, 2), g_att_out[l]).reshape(n_dec, D_ATT)
    sc = state_conv[l]
    lru_s, h_s = _lru_step(xls, gls, sc[:, 0], sc[:, 1], sc[:, 2], state_h[l], *lru_w)
    x1s, xns = _out_proj(att_s, lru_s, xs, wo, g_ffn[l], tm=n_dec)

    y_prompt, y_sample = _ffn(xn, x1, xns, x1s, wu, wd, g_final, tm=FFN_ROWS, tf=FFN_COLS)
    y_prompt = y_prompt.reshape(n_seq, seq, D_MODEL)
    y_sample = y_sample.reshape(n_dec, 1, D_MODEL)

    new_k_s = ks.reshape(1, n_dec, 1, N_HEADS, HEAD_DIM)
    new_v_s = vs.reshape(1, n_dec, 1, N_HEADS, HEAD_DIM)
    new_f_s = lf_s.reshape(1, n_dec, 1, N_HEADS)
    new_c_s = jnp.stack([sc[:, 1], sc[:, 2], xls], axis=1)[None]
    new_h_s = h_s[None]

    return (y_prompt, y_sample, new_k_p, new_v_p, new_f_p, new_c_p, new_h_p,
            new_k_s, new_v_s, new_f_s, new_c_s, new_h_s)
```
